```python
import math
import jax, jax.numpy as jnp
from jax import lax
import numpy as np

D_MODEL = 1024
BATCH = 4
SEQ = 4096
DEPTH = 2

N_MEM = 256
Q_BLOCK = 128
EPS = 1e-6
LRU_WIDTH = D_MODEL // 2
LRU_BLOCKS = 8
LRU_BLOCK_DIM = LRU_WIDTH // LRU_BLOCKS
CONV_WIDTH = 4
LRU_C = 8.0
FOX_HEADS = 8
FOX_HEAD_DIM = (D_MODEL // 2) // FOX_HEADS
FOX_WIDTH = FOX_HEADS * FOX_HEAD_DIM
EVEN_SPLITS = [LRU_WIDTH, LRU_WIDTH, FOX_WIDTH, FOX_WIDTH, FOX_WIDTH, FOX_HEADS]
EVEN_IN = sum(EVEN_SPLITS)
EVEN_MIX = LRU_WIDTH + FOX_WIDTH
DIFF_HEADS = 8
DIFF_HEAD_DIM = D_MODEL // (2 * DIFF_HEADS)
DIFF_V_DIM = 2 * DIFF_HEAD_DIM
DIFF_WIDTH = DIFF_HEADS * DIFF_V_DIM
ODD_IN = 3 * DIFF_WIDTH
XATTN_HEADS = 4
XATTN_HEAD_DIM = D_MODEL // XATTN_HEADS
D_FF = 4 * D_MODEL
N_EVEN = (DEPTH + 1) // 2
N_ODD = DEPTH // 2

kernel_name = "hybrid_rglru_fox_diffattn_trunk"


def _rmsnorm(x, g):
    xf = x.astype(jnp.float32)
    y = xf * lax.rsqrt(jnp.mean(xf * xf, axis=-1, keepdims=True) + EPS)
    return (y * g.astype(jnp.float32)).astype(x.dtype)


def _sweep_query_blocks(block_fn, seq_len):
    starts = jnp.arange(seq_len // Q_BLOCK) * Q_BLOCK
    out = lax.map(block_fn, starts)
    n, b, h, qb, dv = out.shape
    return out.transpose(1, 2, 0, 3, 4).reshape(b, h, n * qb, dv)


def _rglru_group(xb, gb, conv_w, conv_b, w_r, b_r, w_i, b_i, lam):
    b, s, w = xb.shape
    xp = jnp.pad(xb, ((0, 0), (CONV_WIDTH - 1, 0), (0, 0)))
    xc = conv_b + sum(xp[:, k:k + s] * conv_w[k] for k in range(CONV_WIDTH))
    xg = xc.reshape(b, s, LRU_BLOCKS, LRU_BLOCK_DIM)
    r = jax.nn.sigmoid(jnp.einsum('bsgi,gij->bsgj', xg, w_r).reshape(b, s, w) + b_r)
    i = jax.nn.sigmoid(jnp.einsum('bsgi,gij->bsgj', xg, w_i).reshape(b, s, w) + b_i)
    log_a = LRU_C * r.astype(jnp.float32) * jax.nn.log_sigmoid(lam.astype(jnp.float32))
    a = jnp.exp(log_a)
    u = jnp.sqrt(-jnp.expm1(2.0 * log_a)) * (i * xc).astype(jnp.float32)

    def combine(left, right):
        a1, b1 = left
        a2, b2 = right
        return a1 * a2, a2 * b1 + b2

    _, h = lax.associative_scan(combine, (a, u), axis=1)
    return h.astype(xb.dtype) * jax.nn.gelu(gb)


def _fox_attention(q, k, v, log_f):
    b, s, h, dh = q.shape
    qf = (q.astype(jnp.float32) * dh ** -0.5).transpose(0, 2, 1, 3)
    kf = k.astype(jnp.float32).transpose(0, 2, 1, 3)
    vf = v.astype(jnp.float32).transpose(0, 2, 1, 3)
    cum = jnp.cumsum(log_f.astype(jnp.float32), axis=1).transpose(0, 2, 1)
    k_pos = jnp.arange(s)

    def block(start):
        qb = lax.dynamic_slice_in_dim(qf, start, Q_BLOCK, axis=2)
        cq = lax.dynamic_slice_in_dim(cum, start, Q_BLOCK, axis=2)
        q_pos = start + jnp.arange(Q_BLOCK)
        logits = jnp.einsum('bhqd,bhkd->bhqk', qb, kf) + cq[..., :, None] - cum[..., None, :]
        logits = jnp.where(k_pos[None, :] <= q_pos[:, None], logits, -jnp.inf)
        p = jax.nn.softmax(logits, axis=-1)
        return jnp.einsum('bhqk,bhkd->bhqd', p, vf)

    o = _sweep_query_blocks(block, s)
    return o.transpose(0, 2, 1, 3).reshape(b, s, h * dh).astype(q.dtype)


def _diff_attention(q, k, v, lam, lambda_init, norm_g):
    b, s, h, _, dh = q.shape
    dv = v.shape[-1]
    qf = (q.astype(jnp.float32) * dh ** -0.5).transpose(0, 2, 3, 1, 4)
    kf = k.astype(jnp.float32).transpose(0, 2, 3, 1, 4)
    vf = v.astype(jnp.float32).transpose(0, 2, 1, 3)
    k_pos = jnp.arange(s)

    def block(start):
        qb = lax.dynamic_slice_in_dim(qf, start, Q_BLOCK, axis=3)
        q_pos = start + jnp.arange(Q_BLOCK)
        logits = jnp.einsum('bhcqd,bhckd->bhcqk', qb, kf)
        logits = jnp.where(k_pos[None, :] <= q_pos[:, None], logits, -jnp.inf)
        p = jax.nn.softmax(logits, axis=-1)
        w = p[:, :, 0] - lam * p[:, :, 1]
        return jnp.einsum('bhqk,bhkd->bhqd', w, vf)

    o = _sweep_query_blocks(block, s)
    o = o * lax.rsqrt(jnp.mean(o * o, axis=-1, keepdims=True) + EPS)
    o = o * norm_g.astype(jnp.float32) * (1.0 - lambda_init)
    return o.transpose(0, 2, 1, 3).reshape(b, s, h * dv).astype(q.dtype)


def _memory_cross_attention(xn, mem_n, wq, wkv, wo):
    b, s, d = xn.shape
    m = mem_n.shape[1]
    q = (xn @ wq).reshape(b, s, XATTN_HEADS, XATTN_HEAD_DIM).astype(jnp.float32)
    kv = (mem_n @ wkv).reshape(b, m, 2, XATTN_HEADS, XATTN_HEAD_DIM).astype(jnp.float32)
    logits = jnp.einsum('bshd,bmhd->bhsm', q, kv[:, :, 0]) * XATTN_HEAD_DIM ** -0.5
    p = jax.nn.softmax(logits, axis=-1)
    o = jnp.einsum('bhsm,bmhd->bshd', p, kv[:, :, 1]).reshape(b, s, d).astype(xn.dtype)
    return o @ wo


def _sq_relu_mlp(xn, w_up, w_down):
    return jnp.square(jax.nn.relu(xn @ w_up)) @ w_down


def setup_inputs(seed: int = 0) -> dict:
    key = jax.random.key(seed)
    ks = iter(jax.random.split(key, 40))

    def nrm(shape, fan_in):
        return jax.random.normal(next(ks), shape, jnp.float32) * fan_in ** -0.5

    def gain(shape):
        return 1.0 + 0.02 * jax.random.normal(next(ks), shape, jnp.float32)

    def small(shape, scale=0.02):
        return scale * jax.random.normal(next(ks), shape, jnp.float32)

    u = jax.random.uniform(next(ks), (N_EVEN, LRU_WIDTH), jnp.float32, 0.9, 0.999)
    return {
        "x": jax.random.normal(next(ks), (BATCH, SEQ, D_MODEL), jnp.float32),
        "mem": jax.random.normal(next(ks), (BATCH, N_MEM, D_MODEL), jnp.float32),
        "g_mem": gain((D_MODEL,)),
        "g_final": gain((D_MODEL,)),
        "mix_norm_g": gain((DEPTH, D_MODEL)),
        "xattn_norm_g": gain((DEPTH, D_MODEL)),
        "mlp_norm_g": gain((DEPTH, D_MODEL)),
        "w_in_even": nrm((N_EVEN, D_MODEL, EVEN_IN), D_MODEL),
        "conv_w": nrm((N_EVEN, CONV_WIDTH, LRU_WIDTH), CONV_WIDTH),
        "conv_b": small((N_EVEN, LRU_WIDTH)),
        "w_rgate": nrm((N_EVEN, LRU_BLOCKS, LRU_BLOCK_DIM, LRU_BLOCK_DIM), LRU_BLOCK_DIM),
        "b_rgate": small((N_EVEN, LRU_WIDTH)),
        "w_igate": nrm((N_EVEN, LRU_BLOCKS, LRU_BLOCK_DIM, LRU_BLOCK_DIM), LRU_BLOCK_DIM),
        "b_igate": small((N_EVEN, LRU_WIDTH)),
        "lru_lambda": jnp.log(u) - jnp.log1p(-u),
        "fox_forget_b": 3.0 + small((N_EVEN, FOX_HEADS), 0.1),
        "w_out_even": nrm((N_EVEN, EVEN_MIX, D_MODEL), EVEN_MIX),
        "w_in_odd": nrm((N_ODD, D_MODEL, ODD_IN), D_MODEL),
        "lambda_q1": small((N_ODD, DIFF_HEAD_DIM), 0.1),
        "lambda_k1": small((N_ODD, DIFF_HEAD_DIM), 0.1),
        "lambda_q2": small((N_ODD, DIFF_HEAD_DIM), 0.1),
        "lambda_k2": small((N_ODD, DIFF_HEAD_DIM), 0.1),
        "diff_norm_g": gain((N_ODD, DIFF_V_DIM)),
        "w_out_odd": nrm((N_ODD, DIFF_WIDTH, D_MODEL), DIFF_WIDTH),
        "xattn_wq": nrm((DEPTH, D_MODEL, D_MODEL), D_MODEL),
        "xattn_wkv": nrm((DEPTH, D_MODEL, 2 * D_MODEL), D_MODEL),
        "xattn_wo": nrm((DEPTH, D_MODEL, D_MODEL), D_MODEL),
        "w_up": nrm((DEPTH, D_MODEL, D_FF), D_MODEL),
        "w_down": nrm((DEPTH, D_FF, D_MODEL), D_FF),
    }


def reference(x, mem, g_mem, g_final, mix_norm_g, xattn_norm_g, mlp_norm_g,
              w_in_even, conv_w, conv_b, w_rgate, b_rgate, w_igate, b_igate,
              lru_lambda, fox_forget_b, w_out_even,
              w_in_odd, lambda_q1, lambda_k1, lambda_q2, lambda_k2, diff_norm_g, w_out_odd,
              xattn_wq, xattn_wkv, xattn_wo, w_up, w_down):
    b, s, d = x.shape
    split_at = [int(c) for c in np.cumsum(EVEN_SPLITS)[:-1]]
    mem_n = _rmsnorm(mem, g_mem)
    h = x
    for layer in range(DEPTH):
        i = layer // 2
        hn = _rmsnorm(h, mix_norm_g[layer])
        if layer % 2 == 0:
            z = hn @ w_in_even[i]
            xb, gb, q, k, v, f_logit = jnp.split(z, split_at, axis=-1)
            y_lru = _rglru_group(xb, gb, conv_w[i], conv_b[i], w_rgate[i], b_rgate[i],
                                 w_igate[i], b_igate[i], lru_lambda[i])
            log_f = jax.nn.log_sigmoid((f_logit + fox_forget_b[i]).astype(jnp.float32))
            y_fox = _fox_attention(q.reshape(b, s, FOX_HEADS, FOX_HEAD_DIM),
                                   k.reshape(b, s, FOX_HEADS, FOX_HEAD_DIM),
                                   v.reshape(b, s, FOX_HEADS, FOX_HEAD_DIM), log_f)
            mixed = jnp.concatenate([y_lru, y_fox], axis=-1) @ w_out_even[i]
        else:
            z = hn @ w_in_odd[i]
            q, k, v = jnp.split(z, 3, axis=-1)
            lambda_init = 0.8 - 0.6 * math.exp(-0.3 * layer)
            f32 = jnp.float32
            lam = (jnp.exp(jnp.sum(lambda_q1[i].astype(f32) * lambda_k1[i].astype(f32)))
                   - jnp.exp(jnp.sum(lambda_q2[i].astype(f32) * lambda_k2[i].astype(f32)))
                   + lambda_init)
            y = _diff_attention(q.reshape(b, s, DIFF_HEADS, 2, DIFF_HEAD_DIM),
                                k.reshape(b, s, DIFF_HEADS, 2, DIFF_HEAD_DIM),
                                v.reshape(b, s, DIFF_HEADS, DIFF_V_DIM),
                                lam, lambda_init, diff_norm_g[i])
            mixed = y @ w_out_odd[i]
        h = h + mixed
        h = h + _memory_cross_attention(_rmsnorm(h, xattn_norm_g[layer]), mem_n,
                                        xattn_wq[layer], xattn_wkv[layer], xattn_wo[layer])
        h = h + _sq_relu_mlp(_rmsnorm(h, mlp_norm_g[layer]), w_up[layer], w_down[layer])
    return _rmsnorm(h, g_final)
```

```python
import functools
import math

import jax
import jax.numpy as jnp
from jax import lax
from jax.experimental import pallas as pl
from jax.experimental.pallas import tpu as pltpu

F32 = jnp.float32
BF16 = jnp.bfloat16

LANES = 128
SUBLANES = 8
VMEM_LIMIT = 56 * 1024 * 1024

EPS = 1e-6
N_MEM = 256
LRU_WIDTH = 512
LRU_BLOCKS = 8
CONV_WIDTH = 4
LRU_C = 8.0
HEAD_DIM = 64
FOX_HEADS = 8
DIFF_HEADS = 8
XATTN_HEADS = 4

ROW_TILE = 512
ATTN_TILE = 512
LRU_TILE = 256


def _params(*sem):
    return pltpu.CompilerParams(dimension_semantics=sem, vmem_limit_bytes=VMEM_LIMIT)


def _rms(x, g):
    return x * lax.rsqrt(jnp.mean(x * x, axis=-1, keepdims=True) + EPS) * g


def _dot(a, b):
    return jnp.dot(a, b, preferred_element_type=F32)


def _dot_nt(a, b):
    return lax.dot_general(a, b, (((1,), (1,)), ((), ())), preferred_element_type=F32)


def _log_sigmoid(x):
    return jnp.minimum(x, 0.0) - jnp.log1p(jnp.exp(-jnp.abs(x)))


def _expm1(x):
    u = jnp.exp(x)
    near = jnp.where(u == 1.0, x, (u - 1.0) * x / jnp.log(u))
    return jnp.where(jnp.abs(x) < 0.5, near, u - 1.0)


def _const_spec(shape):
    nd = len(shape)
    return pl.BlockSpec(shape, lambda *_: (0,) * nd)


def _even_proj_kernel(x_ref, g_ref, w_ref, fb_ref, zl_ref, qkv_ref, cum_ref, carry_ref):
    si = pl.program_id(1)
    tm = x_ref.shape[1]
    xn = _rms(x_ref[0], g_ref[...]).astype(BF16)
    zl_ref[0] = _dot(xn, w_ref[:, 0:1024])
    qkv_ref[0] = _dot(xn, w_ref[:, 1024:2560]).astype(BF16)
    fl = _dot(xn, w_ref[:, 2560:2688])
    lf = _log_sigmoid(fl.T[0:FOX_HEADS, :] + fb_ref[...])

    @pl.when(si == 0)
    def _():
        carry_ref[...] = jnp.zeros_like(carry_ref)

    pos = lax.broadcasted_iota(jnp.int32, lf.shape, 1)
    shift = 1
    while shift < tm:
        lf = lf + jnp.where(pos >= shift, pltpu.roll(lf, shift, 1), 0.0)
        shift *= 2
    cum = lf + carry_ref[:, 0:1]
    cum_ref[0] = cum
    carry_ref[...] = jnp.broadcast_to(cum[:, tm - 1:tm], carry_ref.shape)


def _even_proj(h, g, w, fb):
    b, s, d = h.shape
    tm = ROW_TILE
    n = w.shape[1]
    return pl.pallas_call(
        _even_proj_kernel,
        grid=(b, s // tm),
        in_specs=[
            pl.BlockSpec((1, tm, d), lambda i, j: (i, j, 0)),
            _const_spec((1, d)),
            _const_spec((d, n)),
            _const_spec((FOX_HEADS, 1)),
        ],
        out_specs=[
            pl.BlockSpec((1, tm, 1024), lambda i, j: (i, j, 0)),
            pl.BlockSpec((1, tm, 1536), lambda i, j: (i, j, 0)),
            pl.BlockSpec((1, FOX_HEADS, tm), lambda i, j: (i, 0, j)),
        ],
        out_shape=[
            jax.ShapeDtypeStruct((b, s, 1024), F32),
            jax.ShapeDtypeStruct((b, s, 1536), BF16),
            jax.ShapeDtypeStruct((b, FOX_HEADS, s), F32),
        ],
        scratch_shapes=[pltpu.VMEM((FOX_HEADS, LANES), F32)],
        compiler_params=_params("parallel", "arbitrary"),
        name="even_proj",
    )(h, g, w, fb)


def _norm_proj_kernel(x_ref, g_ref, w_ref, o_ref):
    xn = _rms(x_ref[...], g_ref[...]).astype(BF16)
    o_ref[...] = _dot(xn, w_ref[...]).astype(o_ref.dtype)


def _norm_proj(x2d, g, w, out_dtype):
    m, d = x2d.shape
    n = w.shape[1]
    tm = min(ROW_TILE, m)
    return pl.pallas_call(
        _norm_proj_kernel,
        grid=(m // tm,),
        in_specs=[pl.BlockSpec((tm, d), lambda i: (i, 0)), _const_spec((1, d)), _const_spec((d, n))],
        out_specs=pl.BlockSpec((tm, n), lambda i: (i, 0)),
        out_shape=jax.ShapeDtypeStruct((m, n), out_dtype),
        compiler_params=_params("parallel"),
        name="norm_proj",
    )(x2d, g, w)


def _gelu_tanh(x):
    return 0.5 * x * (1.0 + jnp.tanh(math.sqrt(2.0 / math.pi) * (x + 0.044715 * (x * x * x))))


def _lru_kernel(xb_ref, gb_ref, cw_ref, cb_ref, wg_ref, bg_ref, lam_ref, y_ref,
                xbuf, a_s, u_s, hc):
    ti = pl.program_id(1)
    ts = xb_ref.shape[1]
    w = LRU_WIDTH

    @pl.when(ti == 0)
    def _():
        xbuf[0:SUBLANES, :] = jnp.zeros((SUBLANES, w), F32)
        hc[...] = jnp.zeros_like(hc)

    xbuf[SUBLANES:SUBLANES + ts, :] = xb_ref[0]
    xc = cb_ref[...]
    for k in range(CONV_WIDTH):
        lag = CONV_WIDTH - 1 - k
        xc = xc + xbuf[SUBLANES - lag:SUBLANES - lag + ts, :] * cw_ref[k:k + 1, :]
    xbuf[0:SUBLANES, :] = xbuf[ts:ts + SUBLANES, :]

    gates = _dot(xc.astype(BF16), wg_ref[...]) + bg_ref[...]
    r = jax.nn.sigmoid(gates[:, 0:w])
    i = jax.nn.sigmoid(gates[:, w:2 * w])
    log_a = LRU_C * r * _log_sigmoid(lam_ref[...])
    a_s[...] = jnp.exp(log_a)
    u_s[...] = jnp.sqrt(-_expm1(2.0 * log_a)) * (i * xc)

    row = lax.broadcasted_iota(jnp.int32, (SUBLANES, w), 0)

    def group(gidx, carry):
        off = pl.multiple_of(gidx * SUBLANES, SUBLANES)
        a = a_s[pl.ds(off, SUBLANES), :]
        u = u_s[pl.ds(off, SUBLANES), :]
        for sh in (1, 2, 4):
            keep = row >= sh
            u = u + a * jnp.where(keep, pltpu.roll(u, sh, 0), 0.0)
            a = a * jnp.where(keep, pltpu.roll(a, sh, 0), 1.0)
        hgrp = u + a * carry
        u_s[pl.ds(off, SUBLANES), :] = hgrp
        return jnp.broadcast_to(hgrp[SUBLANES - 1:SUBLANES, :], (SUBLANES, w))

    hc[...] = lax.fori_loop(0, ts // SUBLANES, group, hc[...], unroll=4)
    y_ref[0] = (u_s[...] * _gelu_tanh(gb_ref[0])).astype(y_ref.dtype)


def _lru(zl, cw, cb, wg, bg, lam):
    b, s, _ = zl.shape
    ts = LRU_TILE
    w = LRU_WIDTH
    return pl.pallas_call(
        _lru_kernel,
        grid=(b, s // ts),
        in_specs=[
            pl.BlockSpec((1, ts, w), lambda i, j: (i, j, 0)),
            pl.BlockSpec((1, ts, w), lambda i, j: (i, j, 1)),
            _const_spec((CONV_WIDTH, w)),
            _const_spec((1, w)),
            _const_spec((w, 2 * w)),
            _const_spec((1, 2 * w)),
            _const_spec((1, w)),
        ],
        out_specs=pl.BlockSpec((1, ts, w), lambda i, j: (i, j, 0)),
        out_shape=jax.ShapeDtypeStruct((b, s, w), BF16),
        scratch_shapes=[
            pltpu.VMEM((SUBLANES + ts, w), F32),
            pltpu.VMEM((ts, w), F32),
            pltpu.VMEM((ts, w), F32),
            pltpu.VMEM((SUBLANES, w), F32),
        ],
        compiler_params=_params("parallel", "arbitrary"),
        name="rglru",
    )(zl, zl, cw, cb, wg, bg, lam)


def _attn_kernel(*refs, mode, lambda_init):
    if mode == "fox":
        q_ref, k_ref, v_ref, cum_ref, o_ref, m_s, l_s, acc_s = refs
    else:
        q_ref, k_ref, v_ref, lamv_ref, g_ref, o_ref, m_s, l_s, acc_s = refs
    qi = pl.program_id(2)
    tq = q_ref.shape[1]
    tk = tq

    lane = lax.broadcasted_iota(jnp.int32, (tq, LANES), 1)
    q = q_ref[0] * (HEAD_DIM ** -0.5)
    zero = jnp.zeros_like(q)
    qs = (jnp.where(lane < HEAD_DIM, q, zero), jnp.where(lane >= HEAD_DIM, q, zero))

    m_s[...] = jnp.full(m_s.shape, -jnp.inf, F32)
    l_s[...] = jnp.zeros(l_s.shape, F32)
    acc_s[...] = jnp.zeros(acc_s.shape, F32)

    def step(j, masked):
        off = pl.multiple_of(j * tk, tk)
        kb = k_ref[0, pl.ds(off, tk), :]
        vb = v_ref[0, pl.ds(off, tk), :]
        for e in range(2):
            s = _dot_nt(qs[e], kb)
            if mode == "fox":
                s = s - cum_ref[0, 0, e:e + 1, pl.ds(off, tk)]
            if masked:
                rows = lax.broadcasted_iota(jnp.int32, (tq, tk), 0)
                cols = lax.broadcasted_iota(jnp.int32, (tq, tk), 1)
                s = jnp.where(cols <= rows, s, -jnp.inf)
            m_prev = m_s[e]
            m_new = jnp.maximum(m_prev, jnp.max(s, axis=1, keepdims=True))
            alpha = jnp.exp(m_prev - m_new)
            p = jnp.exp(s - m_new)
            l_s[e] = alpha * l_s[e] + jnp.sum(p, axis=1, keepdims=True)
            acc_s[e] = alpha * acc_s[e] + _dot(p.astype(BF16), vb)
            m_s[e] = m_new

    def body(j, c):
        step(j, False)
        return c

    lax.fori_loop(0, qi, body, 0)
    step(qi, True)

    o0 = acc_s[0] / l_s[0]
    o1 = acc_s[1] / l_s[1]
    if mode == "fox":
        o = jnp.where(lane < HEAD_DIM, o0, o1)
    else:
        lv = lamv_ref[...]
        lam = (jnp.exp(jnp.sum(lv[0:1] * lv[1:2], axis=1, keepdims=True))
               - jnp.exp(jnp.sum(lv[2:3] * lv[3:4], axis=1, keepdims=True)) + lambda_init)
        o = o0 - lam * o1
        o = o * lax.rsqrt(jnp.mean(o * o, axis=-1, keepdims=True) + EPS)
        o = o * g_ref[...] * (1.0 - lambda_init)
    o_ref[0] = o.astype(o_ref.dtype)


def _attention(mode, qkv, n_groups, extra, lambda_init=0.0):
    b, s, _ = qkv.shape
    t = ATTN_TILE
    in_specs = [
        pl.BlockSpec((1, t, LANES), lambda i, h, j: (i, j, h)),
        pl.BlockSpec((1, s, LANES), lambda i, h, j: (i, 0, n_groups + h)),
        pl.BlockSpec((1, s, LANES), lambda i, h, j: (i, 0, 2 * n_groups + h)),
    ]
    if mode == "fox":
        (cum,) = extra
        in_specs.append(pl.BlockSpec((1, 1, 2, s), lambda i, h, j: (i, h, 0, 0)))
    else:
        lamv, g = extra
        in_specs += [_const_spec(lamv.shape), _const_spec(g.shape)]
    return pl.pallas_call(
        functools.partial(_attn_kernel, mode=mode, lambda_init=lambda_init),
        grid=(b, n_groups, s // t),
        in_specs=in_specs,
        out_specs=pl.BlockSpec((1, t, LANES), lambda i, h, j: (i, j, h)),
        out_shape=jax.ShapeDtypeStruct((b, s, n_groups * LANES), BF16),
        scratch_shapes=[
            pltpu.VMEM((2, t, 1), F32),
            pltpu.VMEM((2, t, 1), F32),
            pltpu.VMEM((2, t, LANES), F32),
        ],
        compiler_params=_params("parallel", "parallel", "arbitrary"),
        name=mode + "_attn",
    )(qkv, qkv, qkv, *extra)


def _out_proj_kernel(*refs, n_in):
    h_ref = refs[0]
    ys = refs[1:1 + n_in]
    ws = refs[1 + n_in:1 + 2 * n_in]
    o_ref = refs[1 + 2 * n_in]
    acc = h_ref[...]
    for y_ref, w_ref in zip(ys, ws):
        acc = acc + _dot(y_ref[...], w_ref[...])
    o_ref[...] = acc


def _out_proj(h2d, ys, ws):
    m, d = h2d.shape
    tm = ROW_TILE
    n_in = len(ys)
    in_specs = [pl.BlockSpec((tm, d), lambda i: (i, 0))]
    in_specs += [pl.BlockSpec((tm, y.shape[1]), lambda i: (i, 0)) for y in ys]
    in_specs += [_const_spec(w.shape) for w in ws]
    return pl.pallas_call(
        functools.partial(_out_proj_kernel, n_in=n_in),
        grid=(m // tm,),
        in_specs=in_specs,
        out_specs=pl.BlockSpec((tm, d), lambda i: (i, 0)),
        out_shape=jax.ShapeDtypeStruct((m, d), F32),
        compiler_params=_params("parallel"),
        name="out_proj",
    )(h2d, *ys, *ws)


def _xattn_kernel(h_ref, g_ref, wq_ref, k_ref, v_ref, wo_ref, o_ref):
    d = h_ref.shape[2]
    dh = d // XATTN_HEADS
    h = h_ref[0]
    q = _dot(_rms(h, g_ref[...]).astype(BF16), wq_ref[...]).astype(BF16)
    outs = []
    for e in range(XATTN_HEADS):
        sl = slice(e * dh, (e + 1) * dh)
        s = _dot_nt(q[:, sl], k_ref[0, :, sl]) * (dh ** -0.5)
        p = jnp.exp(s - jnp.max(s, axis=1, keepdims=True))
        p = p / jnp.sum(p, axis=1, keepdims=True)
        outs.append(_dot(p.astype(BF16), v_ref[0, :, sl]).astype(BF16))
    o = jnp.concatenate(outs, axis=1)
    o_ref[0] = h + _dot(o, wo_ref[...])


def _xattn(h, g, wq, kv, wo):
    b, s, d = h.shape
    tm = ROW_TILE
    return pl.pallas_call(
        _xattn_kernel,
        grid=(b, s // tm),
        in_specs=[
            pl.BlockSpec((1, tm, d), lambda i, j: (i, j, 0)),
            _const_spec((1, d)),
            _const_spec((d, d)),
            pl.BlockSpec((1, N_MEM, d), lambda i, j: (i, 0, 0)),
            pl.BlockSpec((1, N_MEM, d), lambda i, j: (i, 0, 1)),
            _const_spec((d, d)),
        ],
        out_specs=pl.BlockSpec((1, tm, d), lambda i, j: (i, j, 0)),
        out_shape=jax.ShapeDtypeStruct((b, s, d), F32),
        compiler_params=_params("parallel", "parallel"),
        name="xattn",
    )(h, g, wq, kv, kv, wo)


def _mlp_kernel(h_ref, g_ref, wu_ref, wd_ref, gf_ref, o_ref, *, final_norm, ff_chunk):
    h = h_ref[...]
    xn = _rms(h, g_ref[...]).astype(BF16)
    acc = h
    for c in range(wu_ref.shape[1] // ff_chunk):
        sl = slice(c * ff_chunk, (c + 1) * ff_chunk)
        u = jnp.maximum(_dot(xn, wu_ref[:, sl]), 0.0)
        acc = acc + _dot((u * u).astype(BF16), wd_ref[sl, :])
    if final_norm:
        acc = _rms(acc, gf_ref[...])
    o_ref[...] = acc


def _mlp(h2d, g, wu, wd, gf, final_norm):
    m, d = h2d.shape
    tm = ROW_TILE
    return pl.pallas_call(
        functools.partial(_mlp_kernel, final_norm=final_norm, ff_chunk=1024),
        grid=(m // tm,),
        in_specs=[
            pl.BlockSpec((tm, d), lambda i: (i, 0)),
            _const_spec((1, d)),
            _const_spec(wu.shape),
            _const_spec(wd.shape),
            _const_spec((1, d)),
        ],
        out_specs=pl.BlockSpec((tm, d), lambda i: (i, 0)),
        out_shape=jax.ShapeDtypeStruct((m, d), F32),
        compiler_params=_params("parallel"),
        name="mlp",
    )(h2d, g, wu, wd, gf)


def _block_diag(w):
    g, n, _ = w.shape
    eye = jnp.eye(g, dtype=w.dtype)
    return (eye[:, None, :, None] * w[:, :, None, :]).reshape(g * n, g * n)


def kernel(x, mem, g_mem, g_final, mix_norm_g, xattn_norm_g, mlp_norm_g, w_in_even, conv_w, conv_b, w_rgate, b_rgate, w_igate, b_igate, lru_lambda, fox_forget_b, w_out_even, w_in_odd, lambda_q1, lambda_k1, lambda_q2, lambda_k2, diff_norm_g, w_out_odd, xattn_wq, xattn_wkv, xattn_wo, w_up, w_down):
    b, s, d = x.shape
    depth = mix_norm_g.shape[0]
    m = b * s
    row = lambda v: v.reshape(1, -1).astype(F32)

    h = x
    for layer in range(depth):
        i = layer // 2
        if layer % 2 == 0:
            w_in = jnp.pad(w_in_even[i], ((0, 0), (0, 2688 - w_in_even.shape[2]))).astype(BF16)
            zl, qkv, cum = _even_proj(h, row(mix_norm_g[layer]), w_in,
                                      fox_forget_b[i].reshape(FOX_HEADS, 1).astype(F32))
            wg = jnp.concatenate([_block_diag(w_rgate[i]), _block_diag(w_igate[i])], axis=1).astype(BF16)
            bg = jnp.concatenate([b_rgate[i], b_igate[i]]).reshape(1, -1).astype(F32)
            y_lru = _lru(zl, conv_w[i].astype(F32), row(conv_b[i]), wg, bg, row(lru_lambda[i]))
            y_fox = _attention("fox", qkv, FOX_HEADS // 2, (cum.reshape(b, FOX_HEADS // 2, 2, s),))
            w_out = w_out_even[i].astype(BF16)
            h2 = _out_proj(h.reshape(m, d), [y_lru.reshape(m, -1), y_fox.reshape(m, -1)],
                           [w_out[:LRU_WIDTH], w_out[LRU_WIDTH:]])
        else:
            z = _norm_proj(h.reshape(m, d), row(mix_norm_g[layer]), w_in_odd[i].astype(BF16), BF16)
            lambda_init = 0.8 - 0.6 * math.exp(-0.3 * layer)
            lamv = jnp.stack([lambda_q1[i], lambda_k1[i], lambda_q2[i], lambda_k2[i]]).astype(F32)
            y = _attention("diff", z.reshape(b, s, -1), DIFF_HEADS, (lamv, row(diff_norm_g[i])),
                           lambda_init=lambda_init)
            h2 = _out_proj(h.reshape(m, d), [y.reshape(m, -1)], [w_out_odd[i].astype(BF16)])

        kv = _norm_proj(mem.reshape(b * N_MEM, d), row(g_mem), xattn_wkv[layer].astype(BF16), BF16)
        h3 = _xattn(h2.reshape(b, s, d), row(xattn_norm_g[layer]), xattn_wq[layer].astype(BF16),
                    kv.reshape(b, N_MEM, 2 * d), xattn_wo[layer].astype(BF16))
        h4 = _mlp(h3.reshape(m, d), row(mlp_norm_g[layer]), w_up[layer].astype(BF16),
                  w_down[layer].astype(BF16), row(g_final), final_norm=(layer == depth - 1))
        h = h4.reshape(b, s, d)
    return h
```

```python
import functools
import math

import jax
import jax.numpy as jnp
from jax import lax
from jax.experimental import pallas as pl
from jax.experimental.pallas import tpu as pltpu

F32 = jnp.float32
BF16 = jnp.bfloat16

LANES = 128
SUBLANES = 8
VMEM_LIMIT = 56 * 1024 * 1024

EPS = 1e-6
N_MEM = 256
LRU_WIDTH = 512
LRU_BLOCKS = 8
CONV_WIDTH = 4
LRU_C = 8.0
HEAD_DIM = 64
FOX_HEADS = 8
DIFF_HEADS = 8
XATTN_HEADS = 4

LOG2E = 1.4426950408889634
QK_SCALE = HEAD_DIM ** -0.5 * LOG2E
ONES_ROWS = 16

ROW_TILE = 512
ATTN_TILE = 512
LRU_TILE = 256


def _params(*sem):
    return pltpu.CompilerParams(dimension_semantics=sem, vmem_limit_bytes=VMEM_LIMIT)


def _rms(x, g):
    return x * lax.rsqrt(jnp.mean(x * x, axis=-1, keepdims=True) + EPS) * g


def _dot(a, b):
    return jnp.dot(a, b, preferred_element_type=F32)


def _dot_nt(a, b):
    return lax.dot_general(a, b, (((1,), (1,)), ((), ())), preferred_element_type=F32)


def _log_sigmoid(x):
    return jnp.minimum(x, 0.0) - jnp.log1p(jnp.exp(-jnp.abs(x)))


def _expm1(x):
    u = jnp.exp(x)
    near = jnp.where(u == 1.0, x, (u - 1.0) * x / jnp.log(u))
    return jnp.where(jnp.abs(x) < 0.5, near, u - 1.0)


def _const_spec(shape):
    nd = len(shape)
    return pl.BlockSpec(shape, lambda *_: (0,) * nd)


def _even_proj_kernel(x_ref, g_ref, w_ref, fb_ref, zl_ref, qkv_ref, kx_ref, carry_ref):
    si = pl.program_id(1)
    tm = x_ref.shape[1]
    xn = _rms(x_ref[0], g_ref[...]).astype(BF16)
    zl_ref[0] = _dot(xn, w_ref[:, 0:1024])
    qkv_ref[0, :, 0:512] = (_dot(xn, w_ref[:, 1024:1536]) * QK_SCALE).astype(BF16)
    qkv_ref[0, :, 512:1536] = _dot(xn, w_ref[:, 1536:2560]).astype(BF16)
    fl = _dot(xn, w_ref[:, 2560:2688])
    lf = _log_sigmoid(fl.T[0:FOX_HEADS, :] + fb_ref[...])

    @pl.when(si == 0)
    def _():
        carry_ref[...] = jnp.zeros_like(carry_ref)

    pos = lax.broadcasted_iota(jnp.int32, lf.shape, 1)
    shift = 1
    while shift < tm:
        lf = lf + jnp.where(pos >= shift, pltpu.roll(lf, shift, 1), 0.0)
        shift *= 2
    cum = lf + carry_ref[:, 0:1]
    carry_ref[...] = jnp.broadcast_to(cum[:, tm - 1:tm], carry_ref.shape)

    c = cum * (-LOG2E)
    hi = c.astype(BF16).astype(F32)
    mid = (c - hi).astype(BF16).astype(F32)
    lo = (c - hi - mid).astype(BF16).astype(F32)
    pieces = jnp.concatenate([hi, mid, lo, jnp.zeros((LANES - 3 * FOX_HEADS, tm), F32)], axis=0)
    kx_ref[0] = pieces.T.astype(BF16)


def _even_proj(h, g, w, fb):
    b, s, d = h.shape
    tm = ROW_TILE
    n = w.shape[1]
    return pl.pallas_call(
        _even_proj_kernel,
        grid=(b, s // tm),
        in_specs=[
            pl.BlockSpec((1, tm, d), lambda i, j: (i, j, 0)),
            _const_spec((1, d)),
            _const_spec((d, n)),
            _const_spec((FOX_HEADS, 1)),
        ],
        out_specs=[
            pl.BlockSpec((1, tm, 1024), lambda i, j: (i, j, 0)),
            pl.BlockSpec((1, tm, 1536), lambda i, j: (i, j, 0)),
            pl.BlockSpec((1, tm, LANES), lambda i, j: (i, j, 0)),
        ],
        out_shape=[
            jax.ShapeDtypeStruct((b, s, 1024), F32),
            jax.ShapeDtypeStruct((b, s, 1536), BF16),
            jax.ShapeDtypeStruct((b, s, LANES), BF16),
        ],
        scratch_shapes=[pltpu.VMEM((FOX_HEADS, LANES), F32)],
        compiler_params=_params("parallel", "arbitrary"),
        name="even_proj",
    )(h, g, w, fb)


def _norm_proj_kernel(x_ref, g_ref, w_ref, o_ref, *, q_cols):
    xn = _rms(x_ref[...], g_ref[...]).astype(BF16)
    if q_cols:
        o_ref[:, 0:q_cols] = (_dot(xn, w_ref[:, 0:q_cols]) * QK_SCALE).astype(o_ref.dtype)
    o_ref[:, q_cols:] = _dot(xn, w_ref[:, q_cols:]).astype(o_ref.dtype)


def _norm_proj(x2d, g, w, out_dtype, q_cols=0):
    m, d = x2d.shape
    n = w.shape[1]
    tm = min(ROW_TILE, m)
    return pl.pallas_call(
        functools.partial(_norm_proj_kernel, q_cols=q_cols),
        grid=(m // tm,),
        in_specs=[pl.BlockSpec((tm, d), lambda i: (i, 0)), _const_spec((1, d)), _const_spec((d, n))],
        out_specs=pl.BlockSpec((tm, n), lambda i: (i, 0)),
        out_shape=jax.ShapeDtypeStruct((m, n), out_dtype),
        compiler_params=_params("parallel"),
        name="norm_proj",
    )(x2d, g, w)


def _gelu_tanh(x):
    return 0.5 * x * (1.0 + jnp.tanh(math.sqrt(2.0 / math.pi) * (x + 0.044715 * (x * x * x))))


def _lru_kernel(xb_ref, gb_ref, cw_ref, cb_ref, wg_ref, bg_ref, lam_ref, y_ref,
                xbuf, a_s, u_s, hc):
    ti = pl.program_id(1)
    ts = xb_ref.shape[1]
    w = LRU_WIDTH

    @pl.when(ti == 0)
    def _():
        xbuf[0:SUBLANES, :] = jnp.zeros((SUBLANES, w), F32)
        hc[...] = jnp.zeros_like(hc)

    xbuf[SUBLANES:SUBLANES + ts, :] = xb_ref[0]
    xc = cb_ref[...]
    for k in range(CONV_WIDTH):
        lag = CONV_WIDTH - 1 - k
        xc = xc + xbuf[SUBLANES - lag:SUBLANES - lag + ts, :] * cw_ref[k:k + 1, :]
    xbuf[0:SUBLANES, :] = xbuf[ts:ts + SUBLANES, :]

    gates = _dot(xc.astype(BF16), wg_ref[...]) + bg_ref[...]
    r = jax.nn.sigmoid(gates[:, 0:w])
    i = jax.nn.sigmoid(gates[:, w:2 * w])
    log_a = LRU_C * r * _log_sigmoid(lam_ref[...])
    a_s[...] = jnp.exp(log_a)
    u_s[...] = jnp.sqrt(-_expm1(2.0 * log_a)) * (i * xc)

    row = lax.broadcasted_iota(jnp.int32, (SUBLANES, w), 0)

    def group(gidx, carry):
        off = pl.multiple_of(gidx * SUBLANES, SUBLANES)
        a = a_s[pl.ds(off, SUBLANES), :]
        u = u_s[pl.ds(off, SUBLANES), :]
        for sh in (1, 2, 4):
            keep = row >= sh
            u = u + a * jnp.where(keep, pltpu.roll(u, sh, 0), 0.0)
            a = a * jnp.where(keep, pltpu.roll(a, sh, 0), 1.0)
        hgrp = u + a * carry
        u_s[pl.ds(off, SUBLANES), :] = hgrp
        return jnp.broadcast_to(hgrp[SUBLANES - 1:SUBLANES, :], (SUBLANES, w))

    hc[...] = lax.fori_loop(0, ts // SUBLANES, group, hc[...], unroll=4)
    y_ref[0] = (u_s[...] * _gelu_tanh(gb_ref[0])).astype(y_ref.dtype)


def _lru(zl, cw, cb, wg, bg, lam):
    b, s, _ = zl.shape
    ts = LRU_TILE
    w = LRU_WIDTH
    return pl.pallas_call(
        _lru_kernel,
        grid=(b, s // ts),
        in_specs=[
            pl.BlockSpec((1, ts, w), lambda i, j: (i, j, 0)),
            pl.BlockSpec((1, ts, w), lambda i, j: (i, j, 1)),
            _const_spec((CONV_WIDTH, w)),
            _const_spec((1, w)),
            _const_spec((w, 2 * w)),
            _const_spec((1, 2 * w)),
            _const_spec((1, w)),
        ],
        out_specs=pl.BlockSpec((1, ts, w), lambda i, j: (i, j, 0)),
        out_shape=jax.ShapeDtypeStruct((b, s, w), BF16),
        scratch_shapes=[
            pltpu.VMEM((SUBLANES + ts, w), F32),
            pltpu.VMEM((ts, w), F32),
            pltpu.VMEM((ts, w), F32),
            pltpu.VMEM((SUBLANES, w), F32),
        ],
        compiler_params=_params("parallel", "arbitrary"),
        name="rglru",
    )(zl, zl, cw, cb, wg, bg, lam)


def _attn_kernel(*refs, mode, lambda_init):
    if mode == "fox":
        q_ref, k_ref, v_ref, kx_ref, o_ref, vt_s, s0_s, s1_s, mc_s, m_s, acc_s = refs
    else:
        q_ref, k_ref, v_ref, lamv_ref, g_ref, o_ref, vt_s, s0_s, s1_s, mc_s, m_s, acc_s = refs
    bufs = (s0_s, s1_s)
    hp = pl.program_id(1)
    qi = pl.program_id(2)
    tq = q_ref.shape[1]
    tk = tq
    s_len = k_ref.shape[1]
    dv = HEAD_DIM if mode == "fox" else LANES

    @pl.when(qi == 0)
    def _():
        for c in range(s_len // tk):
            cols = slice(c * tk, (c + 1) * tk)
            vt = v_ref[0, cols, :].astype(F32).T.astype(BF16)
            ones = jnp.ones((ONES_ROWS, tk), BF16)
            if mode == "fox":
                for e in range(2):
                    vt_s[e, 0:dv, cols] = vt[e * dv:(e + 1) * dv]
                    vt_s[e, dv:dv + ONES_ROWS, cols] = ones
            else:
                vt_s[0, 0:dv, cols] = vt
                vt_s[0, dv:dv + ONES_ROWS, cols] = ones

    lane = lax.broadcasted_iota(jnp.int32, (tq, LANES), 1)
    q = q_ref[0]
    zero = jnp.zeros_like(q)
    qs = [jnp.where(lane < HEAD_DIM, q, zero), jnp.where(lane >= HEAD_DIM, q, zero)]
    if mode == "fox":
        for e in range(2):
            head = 2 * hp + e
            pick = (lane == head) | (lane == head + FOX_HEADS) | (lane == head + 2 * FOX_HEADS)
            qs[e] = jnp.concatenate([qs[e], jnp.where(pick, 1.0, 0.0).astype(BF16)], axis=1)

    m_s[...] = jnp.full(m_s.shape, -jnp.inf, F32)
    acc_s[...] = jnp.zeros(acc_s.shape, F32)

    def colmax(x):
        acc = x[0:16]
        for i in range(1, x.shape[0] // 16):
            acc = jnp.maximum(acc, x[i * 16:(i + 1) * 16])
        return jnp.max(acc, axis=0, keepdims=True)

    def stage1(j, e, masked):
        off = pl.multiple_of(j * tk, tk)
        kb = k_ref[0, pl.ds(off, tk), :]
        if mode == "fox":
            kb = jnp.concatenate([kb, kx_ref[0, pl.ds(off, tk), :]], axis=1)
        s = _dot_nt(kb, qs[e])
        if masked:
            keys = lax.broadcasted_iota(jnp.int32, (tk, tq), 0)
            qpos = lax.broadcasted_iota(jnp.int32, (tk, tq), 1)
            s = jnp.where(keys <= qpos, s, -jnp.inf)
        bufs[e][...] = s
        mc_s[e] = colmax(s)

    def stage2(j, e):
        off = pl.multiple_of(j * tk, tk)
        vt = vt_s[e if mode == "fox" else 0, :, pl.ds(off, tk)]
        m_prev = m_s[e]
        m_new = jnp.maximum(m_prev, mc_s[e])
        alpha = jnp.exp2(m_prev - m_new)
        p = jnp.exp2(bufs[e][...] - m_new).astype(BF16)
        acc_s[e] = alpha * acc_s[e] + _dot(vt, p)
        m_s[e] = m_new

    @pl.when(qi == 0)
    def _():
        stage1(0, 0, True)
        stage1(0, 1, True)
        stage2(0, 0)
        stage2(0, 1)

    @pl.when(qi > 0)
    def _():
        stage1(0, 0, False)

        def body(j, c):
            stage1(j, 1, False)
            stage2(j, 0)
            stage1(j + 1, 0, False)
            stage2(j, 1)
            return c

        lax.fori_loop(0, qi - 1, body, 0)
        j = qi - 1
        stage1(j, 1, False)
        stage2(j, 0)
        stage1(qi, 0, True)
        stage2(j, 1)
        stage1(qi, 1, True)
        stage2(qi, 0)
        stage2(qi, 1)

    o0 = acc_s[0, 0:dv] / acc_s[0, dv:dv + 1]
    o1 = acc_s[1, 0:dv] / acc_s[1, dv:dv + 1]
    if mode == "fox":
        o = jnp.concatenate([o0, o1], axis=0).T
    else:
        lv = lamv_ref[...]
        lam = (jnp.exp(jnp.sum(lv[0:1] * lv[1:2], axis=1, keepdims=True))
               - jnp.exp(jnp.sum(lv[2:3] * lv[3:4], axis=1, keepdims=True)) + lambda_init)
        o = (o0 - lam * o1).T
        o = o * lax.rsqrt(jnp.mean(o * o, axis=-1, keepdims=True) + EPS)
        o = o * g_ref[...] * (1.0 - lambda_init)
    o_ref[0] = o.astype(o_ref.dtype)


def _attention(mode, qkv, n_groups, extra, lambda_init=0.0):
    b, s, _ = qkv.shape
    t = ATTN_TILE
    in_specs = [
        pl.BlockSpec((1, t, LANES), lambda i, h, j: (i, j, h)),
        pl.BlockSpec((1, s, LANES), lambda i, h, j: (i, 0, n_groups + h)),
        pl.BlockSpec((1, s, LANES), lambda i, h, j: (i, 0, 2 * n_groups + h)),
    ]
    if mode == "fox":
        (kx,) = extra
        in_specs.append(pl.BlockSpec((1, s, LANES), lambda i, h, j: (i, 0, 0)))
        dv, nv = HEAD_DIM, 2
    else:
        lamv, g = extra
        in_specs += [_const_spec(lamv.shape), _const_spec(g.shape)]
        dv, nv = LANES, 1
    return pl.pallas_call(
        functools.partial(_attn_kernel, mode=mode, lambda_init=lambda_init),
        grid=(b, n_groups, s // t),
        in_specs=in_specs,
        out_specs=pl.BlockSpec((1, t, LANES), lambda i, h, j: (i, j, h)),
        out_shape=jax.ShapeDtypeStruct((b, s, n_groups * LANES), BF16),
        scratch_shapes=[
            pltpu.VMEM((nv, dv + ONES_ROWS, s), BF16),
            pltpu.VMEM((t, t), F32),
            pltpu.VMEM((t, t), F32),
            pltpu.VMEM((2, 1, t), F32),
            pltpu.VMEM((2, 1, t), F32),
            pltpu.VMEM((2, dv + ONES_ROWS, t), F32),
        ],
        compiler_params=_params("parallel", "parallel", "arbitrary"),
        name=mode + "_attn",
    )(qkv, qkv, qkv, *extra)


def _out_proj_kernel(*refs, n_in):
    h_ref = refs[0]
    ys = refs[1:1 + n_in]
    ws = refs[1 + n_in:1 + 2 * n_in]
    o_ref = refs[1 + 2 * n_in]
    acc = h_ref[...]
    for y_ref, w_ref in zip(ys, ws):
        acc = acc + _dot(y_ref[...], w_ref[...])
    o_ref[...] = acc


def _out_proj(h2d, ys, ws):
    m, d = h2d.shape
    tm = ROW_TILE
    n_in = len(ys)
    in_specs = [pl.BlockSpec((tm, d), lambda i: (i, 0))]
    in_specs += [pl.BlockSpec((tm, y.shape[1]), lambda i: (i, 0)) for y in ys]
    in_specs += [_const_spec(w.shape) for w in ws]
    return pl.pallas_call(
        functools.partial(_out_proj_kernel, n_in=n_in),
        grid=(m // tm,),
        in_specs=in_specs,
        out_specs=pl.BlockSpec((tm, d), lambda i: (i, 0)),
        out_shape=jax.ShapeDtypeStruct((m, d), F32),
        compiler_params=_params("parallel"),
        name="out_proj",
    )(h2d, *ys, *ws)


def _xattn_kernel(h_ref, g_ref, wq_ref, k_ref, v_ref, wo_ref, o_ref):
    d = h_ref.shape[2]
    dh = d // XATTN_HEADS
    h = h_ref[0]
    q = _dot(_rms(h, g_ref[...]).astype(BF16), wq_ref[...]).astype(BF16)
    outs = []
    for e in range(XATTN_HEADS):
        sl = slice(e * dh, (e + 1) * dh)
        s = _dot_nt(q[:, sl], k_ref[0, :, sl]) * (dh ** -0.5)
        p = jnp.exp(s - jnp.max(s, axis=1, keepdims=True))
        p = p / jnp.sum(p, axis=1, keepdims=True)
        outs.append(_dot(p.astype(BF16), v_ref[0, :, sl]).astype(BF16))
    o = jnp.concatenate(outs, axis=1)
    o_ref[0] = h + _dot(o, wo_ref[...])


def _xattn(h, g, wq, kv, wo):
    b, s, d = h.shape
    tm = ROW_TILE
    return pl.pallas_call(
        _xattn_kernel,
        grid=(b, s // tm),
        in_specs=[
            pl.BlockSpec((1, tm, d), lambda i, j: (i, j, 0)),
            _const_spec((1, d)),
            _const_spec((d, d)),
            pl.BlockSpec((1, N_MEM, d), lambda i, j: (i, 0, 0)),
            pl.BlockSpec((1, N_MEM, d), lambda i, j: (i, 0, 1)),
            _const_spec((d, d)),
        ],
        out_specs=pl.BlockSpec((1, tm, d), lambda i, j: (i, j, 0)),
        out_shape=jax.ShapeDtypeStruct((b, s, d), F32),
        compiler_params=_params("parallel", "parallel"),
        name="xattn",
    )(h, g, wq, kv, kv, wo)


def _mlp_kernel(h_ref, g_ref, wu_ref, wd_ref, gf_ref, o_ref, *, final_norm, ff_chunk):
    h = h_ref[...]
    xn = _rms(h, g_ref[...]).astype(BF16)
    acc = h
    for c in range(wu_ref.shape[1] // ff_chunk):
        sl = slice(c * ff_chunk, (c + 1) * ff_chunk)
        u = jnp.maximum(_dot(xn, wu_ref[:, sl]), 0.0)
        acc = acc + _dot((u * u).astype(BF16), wd_ref[sl, :])
    if final_norm:
        acc = _rms(acc, gf_ref[...])
    o_ref[...] = acc


def _mlp(h2d, g, wu, wd, gf, final_norm):
    m, d = h2d.shape
    tm = ROW_TILE
    return pl.pallas_call(
        functools.partial(_mlp_kernel, final_norm=final_norm, ff_chunk=1024),
        grid=(m // tm,),
        in_specs=[
            pl.BlockSpec((tm, d), lambda i: (i, 0)),
            _const_spec((1, d)),
            _const_spec(wu.shape),
            _const_spec(wd.shape),
            _const_spec((1, d)),
        ],
        out_specs=pl.BlockSpec((tm, d), lambda i: (i, 0)),
        out_shape=jax.ShapeDtypeStruct((m, d), F32),
        compiler_params=_params("parallel"),
        name="mlp",
    )(h2d, g, wu, wd, gf)


def _block_diag(w):
    g, n, _ = w.shape
    eye = jnp.eye(g, dtype=w.dtype)
    return (eye[:, None, :, None] * w[:, :, None, :]).reshape(g * n, g * n)


def kernel(x, mem, g_mem, g_final, mix_norm_g, xattn_norm_g, mlp_norm_g, w_in_even, conv_w, conv_b, w_rgate, b_rgate, w_igate, b_igate, lru_lambda, fox_forget_b, w_out_even, w_in_odd, lambda_q1, lambda_k1, lambda_q2, lambda_k2, diff_norm_g, w_out_odd, xattn_wq, xattn_wkv, xattn_wo, w_up, w_down):
    b, s, d = x.shape
    depth = mix_norm_g.shape[0]
    m = b * s
    row = lambda v: v.reshape(1, -1).astype(F32)

    h = x
    for layer in range(depth):
        i = layer // 2
        if layer % 2 == 0:
            w_in = jnp.pad(w_in_even[i], ((0, 0), (0, 2688 - w_in_even.shape[2]))).astype(BF16)
            zl, qkv, kx = _even_proj(h, row(mix_norm_g[layer]), w_in,
                                     fox_forget_b[i].reshape(FOX_HEADS, 1).astype(F32))
            wg = jnp.concatenate([_block_diag(w_rgate[i]), _block_diag(w_igate[i])], axis=1).astype(BF16)
            bg = jnp.concatenate([b_rgate[i], b_igate[i]]).reshape(1, -1).astype(F32)
            y_lru = _lru(zl, conv_w[i].astype(F32), row(conv_b[i]), wg, bg, row(lru_lambda[i]))
            y_fox = _attention("fox", qkv, FOX_HEADS // 2, (kx,))
            w_out = w_out_even[i].astype(BF16)
            h2 = _out_proj(h.reshape(m, d), [y_lru.reshape(m, -1), y_fox.reshape(m, -1)],
                           [w_out[:LRU_WIDTH], w_out[LRU_WIDTH:]])
        else:
            z = _norm_proj(h.reshape(m, d), row(mix_norm_g[layer]), w_in_odd[i].astype(BF16), BF16,
                           q_cols=DIFF_HEADS * 2 * HEAD_DIM)
            lambda_init = 0.8 - 0.6 * math.exp(-0.3 * layer)
            lamv = jnp.stack([lambda_q1[i], lambda_k1[i], lambda_q2[i], lambda_k2[i]]).astype(F32)
            y = _attention("diff", z.reshape(b, s, -1), DIFF_HEADS, (lamv, row(diff_norm_g[i])),
                           lambda_init=lambda_init)
            h2 = _out_proj(h.reshape(m, d), [y.reshape(m, -1)], [w_out_odd[i].astype(BF16)])

        kv = _norm_proj(mem.reshape(b * N_MEM, d), row(g_mem), xattn_wkv[layer].astype(BF16), BF16)
        h3 = _xattn(h2.reshape(b, s, d), row(xattn_norm_g[layer]), xattn_wq[layer].astype(BF16),
                    kv.reshape(b, N_MEM, 2 * d), xattn_wo[layer].astype(BF16))
        h4 = _mlp(h3.reshape(m, d), row(mlp_norm_g[layer]), w_up[layer].astype(BF16),
                  w_down[layer].astype(BF16), row(g_final), final_norm=(layer == depth - 1))
        h = h4.reshape(b, s, d)
    return h
```

```python
import functools
import math

import jax
import jax.numpy as jnp
from jax import lax
from jax.experimental import pallas as pl
from jax.experimental.pallas import tpu as pltpu

F32 = jnp.float32
BF16 = jnp.bfloat16

LANES = 128
SUBLANES = 8
VMEM_LIMIT = 56 * 1024 * 1024

EPS = 1e-6
N_MEM = 256
LRU_WIDTH = 512
LRU_BLOCKS = 8
CONV_WIDTH = 4
LRU_C = 8.0
HEAD_DIM = 64
FOX_HEADS = 8
DIFF_HEADS = 8
XATTN_HEADS = 4

LOG2E = 1.4426950408889634
QK_SCALE = HEAD_DIM ** -0.5 * LOG2E
ONES_ROWS = 16

ROW_TILE = 512
ATTN_TILE = 512
LRU_TILE = 256


def _params(*sem):
    return pltpu.CompilerParams(dimension_semantics=sem, vmem_limit_bytes=VMEM_LIMIT)


def _rms(x, g):
    return x * lax.rsqrt(jnp.mean(x * x, axis=-1, keepdims=True) + EPS) * g


def _dot(a, b):
    return jnp.dot(a, b, preferred_element_type=F32)


def _dot_nt(a, b):
    return lax.dot_general(a, b, (((1,), (1,)), ((), ())), preferred_element_type=F32)


def _log_sigmoid(x):
    return jnp.minimum(x, 0.0) - jnp.log1p(jnp.exp(-jnp.abs(x)))


def _expm1(x):
    u = jnp.exp(x)
    near = jnp.where(u == 1.0, x, (u - 1.0) * x / jnp.log(u))
    return jnp.where(jnp.abs(x) < 0.5, near, u - 1.0)


def _const_spec(shape):
    nd = len(shape)
    return pl.BlockSpec(shape, lambda *_: (0,) * nd)


def _even_proj_kernel(x_ref, g_ref, w_ref, fb_ref, zl_ref, qkv_ref, kx_ref, carry_ref):
    si = pl.program_id(1)
    tm = x_ref.shape[1]
    xn = _rms(x_ref[0], g_ref[...]).astype(BF16)
    zl_ref[0] = _dot(xn, w_ref[:, 0:1024])
    qkv_ref[0, :, 0:512] = (_dot(xn, w_ref[:, 1024:1536]) * QK_SCALE).astype(BF16)
    qkv_ref[0, :, 512:1536] = _dot(xn, w_ref[:, 1536:2560]).astype(BF16)
    fl = _dot(xn, w_ref[:, 2560:2688])
    lf = _log_sigmoid(fl.T[0:FOX_HEADS, :] + fb_ref[...])

    @pl.when(si == 0)
    def _():
        carry_ref[...] = jnp.zeros_like(carry_ref)

    pos = lax.broadcasted_iota(jnp.int32, lf.shape, 1)
    shift = 1
    while shift < tm:
        lf = lf + jnp.where(pos >= shift, pltpu.roll(lf, shift, 1), 0.0)
        shift *= 2
    cum = lf + carry_ref[:, 0:1]
    carry_ref[...] = jnp.broadcast_to(cum[:, tm - 1:tm], carry_ref.shape)

    c = cum * (-LOG2E)
    hi = c.astype(BF16).astype(F32)
    mid = (c - hi).astype(BF16).astype(F32)
    lo = (c - hi - mid).astype(BF16).astype(F32)
    pieces = jnp.concatenate([hi, mid, lo, jnp.zeros((LANES - 3 * FOX_HEADS, tm), F32)], axis=0)
    kx_ref[0] = pieces.T.astype(BF16)


def _even_proj(h, g, w, fb):
    b, s, d = h.shape
    tm = ROW_TILE
    n = w.shape[1]
    return pl.pallas_call(
        _even_proj_kernel,
        grid=(b, s // tm),
        in_specs=[
            pl.BlockSpec((1, tm, d), lambda i, j: (i, j, 0)),
            _const_spec((1, d)),
            _const_spec((d, n)),
            _const_spec((FOX_HEADS, 1)),
        ],
        out_specs=[
            pl.BlockSpec((1, tm, 1024), lambda i, j: (i, j, 0)),
            pl.BlockSpec((1, tm, 1536), lambda i, j: (i, j, 0)),
            pl.BlockSpec((1, tm, LANES), lambda i, j: (i, j, 0)),
        ],
        out_shape=[
            jax.ShapeDtypeStruct((b, s, 1024), F32),
            jax.ShapeDtypeStruct((b, s, 1536), BF16),
            jax.ShapeDtypeStruct((b, s, LANES), BF16),
        ],
        scratch_shapes=[pltpu.VMEM((FOX_HEADS, LANES), F32)],
        compiler_params=_params("parallel", "arbitrary"),
        name="even_proj",
    )(h, g, w, fb)


def _norm_proj_kernel(x_ref, g_ref, w_ref, o_ref, *, q_cols):
    xn = _rms(x_ref[...], g_ref[...]).astype(BF16)
    if q_cols:
        o_ref[:, 0:q_cols] = (_dot(xn, w_ref[:, 0:q_cols]) * QK_SCALE).astype(o_ref.dtype)
    o_ref[:, q_cols:] = _dot(xn, w_ref[:, q_cols:]).astype(o_ref.dtype)


def _norm_proj(x2d, g, w, out_dtype, q_cols=0):
    m, d = x2d.shape
    n = w.shape[1]
    tm = min(ROW_TILE, m)
    return pl.pallas_call(
        functools.partial(_norm_proj_kernel, q_cols=q_cols),
        grid=(m // tm,),
        in_specs=[pl.BlockSpec((tm, d), lambda i: (i, 0)), _const_spec((1, d)), _const_spec((d, n))],
        out_specs=pl.BlockSpec((tm, n), lambda i: (i, 0)),
        out_shape=jax.ShapeDtypeStruct((m, n), out_dtype),
        compiler_params=_params("parallel"),
        name="norm_proj",
    )(x2d, g, w)


def _gelu_tanh(x):
    return 0.5 * x * (1.0 + jnp.tanh(math.sqrt(2.0 / math.pi) * (x + 0.044715 * (x * x * x))))


def _lru_kernel(xb_ref, gb_ref, cw_ref, cb_ref, wg_ref, bg_ref, lam_ref, y_ref,
                xbuf, a_s, u_s, hc):
    ti = pl.program_id(1)
    ts = xb_ref.shape[1]
    w = LRU_WIDTH

    @pl.when(ti == 0)
    def _():
        xbuf[0:SUBLANES, :] = jnp.zeros((SUBLANES, w), F32)
        hc[...] = jnp.zeros_like(hc)

    xbuf[SUBLANES:SUBLANES + ts, :] = xb_ref[0]
    xc = cb_ref[...]
    for k in range(CONV_WIDTH):
        lag = CONV_WIDTH - 1 - k
        xc = xc + xbuf[SUBLANES - lag:SUBLANES - lag + ts, :] * cw_ref[k:k + 1, :]
    xbuf[0:SUBLANES, :] = xbuf[ts:ts + SUBLANES, :]

    gates = _dot(xc.astype(BF16), wg_ref[...]) + bg_ref[...]
    r = jax.nn.sigmoid(gates[:, 0:w])
    i = jax.nn.sigmoid(gates[:, w:2 * w])
    log_a = LRU_C * r * _log_sigmoid(lam_ref[...])
    a_s[...] = jnp.exp(log_a)
    u_s[...] = jnp.sqrt(-_expm1(2.0 * log_a)) * (i * xc)

    row = lax.broadcasted_iota(jnp.int32, (SUBLANES, w), 0)

    def group(gidx, carry):
        off = pl.multiple_of(gidx * SUBLANES, SUBLANES)
        a = a_s[pl.ds(off, SUBLANES), :]
        u = u_s[pl.ds(off, SUBLANES), :]
        for sh in (1, 2, 4):
            keep = row >= sh
            u = u + a * jnp.where(keep, pltpu.roll(u, sh, 0), 0.0)
            a = a * jnp.where(keep, pltpu.roll(a, sh, 0), 1.0)
        hgrp = u + a * carry
        u_s[pl.ds(off, SUBLANES), :] = hgrp
        return jnp.broadcast_to(hgrp[SUBLANES - 1:SUBLANES, :], (SUBLANES, w))

    hc[...] = lax.fori_loop(0, ts // SUBLANES, group, hc[...], unroll=4)
    y_ref[0] = (u_s[...] * _gelu_tanh(gb_ref[0])).astype(y_ref.dtype)


def _lru(zl, cw, cb, wg, bg, lam):
    b, s, _ = zl.shape
    ts = LRU_TILE
    w = LRU_WIDTH
    return pl.pallas_call(
        _lru_kernel,
        grid=(b, s // ts),
        in_specs=[
            pl.BlockSpec((1, ts, w), lambda i, j: (i, j, 0)),
            pl.BlockSpec((1, ts, w), lambda i, j: (i, j, 1)),
            _const_spec((CONV_WIDTH, w)),
            _const_spec((1, w)),
            _const_spec((w, 2 * w)),
            _const_spec((1, 2 * w)),
            _const_spec((1, w)),
        ],
        out_specs=pl.BlockSpec((1, ts, w), lambda i, j: (i, j, 0)),
        out_shape=jax.ShapeDtypeStruct((b, s, w), BF16),
        scratch_shapes=[
            pltpu.VMEM((SUBLANES + ts, w), F32),
            pltpu.VMEM((ts, w), F32),
            pltpu.VMEM((ts, w), F32),
            pltpu.VMEM((SUBLANES, w), F32),
        ],
        compiler_params=_params("parallel", "arbitrary"),
        name="rglru",
    )(zl, zl, cw, cb, wg, bg, lam)


def _attn_items(nt):
    items = [(qi, j) for qi in range(1, nt) for j in range(qi)]
    return items + [items[-1]] * 2


def _attn_kernel(*refs, mode, lambda_init):
    if mode == "fox":
        tab_ref, q_ref, k_ref, v_ref, kx_ref, o_ref, vt_s, qt_s, sa0, sa1, sb0, sb1, mc_s, m_s, acc_s = refs
    else:
        tab_ref, q_ref, k_ref, v_ref, lamv_ref, g_ref, o_ref, vt_s, qt_s, sa0, sa1, sb0, sb1, mc_s, m_s, acc_s = refs
    bufs = ((sa0, sa1), (sb0, sb1))
    hp = pl.program_id(1)
    t = ATTN_TILE
    s_len = k_ref.shape[1]
    nt = s_len // t
    dv = HEAD_DIM if mode == "fox" else LANES

    rowi = lax.broadcasted_iota(jnp.int32, (LANES, t), 0)
    for c in range(nt):
        cols = slice(c * t, (c + 1) * t)
        vt = v_ref[0, cols, :].astype(F32).T.astype(BF16)
        ones = jnp.ones((ONES_ROWS, t), BF16)
        if mode == "fox":
            for e in range(2):
                vt_s[e, 0:dv, cols] = vt[e * dv:(e + 1) * dv]
                vt_s[e, dv:dv + ONES_ROWS, cols] = ones
        else:
            vt_s[0, 0:dv, cols] = vt
            vt_s[0, dv:dv + ONES_ROWS, cols] = ones
        qt = q_ref[0, cols, :].astype(F32).T.astype(BF16)
        zero = jnp.zeros_like(qt)
        for e in range(2):
            keep = (rowi < HEAD_DIM) if e == 0 else (rowi >= HEAD_DIM)
            qt_s[e, 0:LANES, cols] = jnp.where(keep, qt, zero)
            if mode == "fox":
                head = 2 * hp + e
                pick = (rowi == head) | (rowi == head + FOX_HEADS) | (rowi == head + 2 * FOX_HEADS)
                qt_s[e, LANES:2 * LANES, cols] = jnp.where(pick, 1.0, 0.0).astype(BF16)

    m_s[...] = jnp.full(m_s.shape, -jnp.inf, F32)
    acc_s[...] = jnp.zeros(acc_s.shape, F32)

    def colmax(x):
        acc = x[0:16]
        for i in range(1, x.shape[0] // 16):
            acc = jnp.maximum(acc, x[i * 16:(i + 1) * 16])
        return jnp.max(acc, axis=0, keepdims=True)

    def stage1(qi, j, par, e, masked):
        off = pl.multiple_of(j * t, t)
        kb = k_ref[0, pl.ds(off, t), :]
        if mode == "fox":
            kb = jnp.concatenate([kb, kx_ref[0, pl.ds(off, t), :]], axis=1)
        s = _dot(kb, qt_s[e, :, pl.ds(pl.multiple_of(qi * t, t), t)])
        if masked:
            keys = lax.broadcasted_iota(jnp.int32, (t, t), 0)
            qpos = lax.broadcasted_iota(jnp.int32, (t, t), 1)
            s = jnp.where(keys <= qpos, s, -jnp.inf)
        bufs[par][e][...] = s
        mc_s[par, e] = colmax(s)

    def stage2(qi, j, par, e):
        off = pl.multiple_of(j * t, t)
        vt = vt_s[e if mode == "fox" else 0, :, pl.ds(off, t)]
        m_prev = m_s[qi, e]
        m_new = jnp.maximum(m_prev, mc_s[par, e])
        alpha = jnp.exp2(m_prev - m_new)
        p = jnp.exp2(bufs[par][e][...] - m_new).astype(BF16)
        acc_s[qi, e] = alpha * acc_s[qi, e] + _dot(vt, p)
        m_s[qi, e] = m_new

    def item(w):
        return tab_ref[w, 0], tab_ref[w, 1]

    qa, ja = item(0)
    stage1(qa, ja, 0, 0, False)
    stage1(qa, ja, 0, 1, False)

    def body(i, c):
        w = 2 * i
        for par in range(2):
            qc, jc = item(w + par)
            qn, jn = item(w + par + 1)
            for e in range(2):
                stage1(qn, jn, 1 - par, e, False)
                stage2(qc, jc, par, e)
        return c

    lax.fori_loop(0, (nt * (nt - 1)) // 4, body, 0)

    stage1(0, 0, 0, 0, True)
    stage1(0, 0, 0, 1, True)
    for qi in range(nt):
        par = qi % 2
        for e in range(2):
            if qi + 1 < nt:
                stage1(qi + 1, qi + 1, 1 - par, e, True)
            stage2(qi, qi, par, e)

    for qi in range(nt):
        o0 = acc_s[qi, 0, 0:dv] / acc_s[qi, 0, dv:dv + 1]
        o1 = acc_s[qi, 1, 0:dv] / acc_s[qi, 1, dv:dv + 1]
        if mode == "fox":
            o = jnp.concatenate([o0, o1], axis=0).T
        else:
            lv = lamv_ref[...]
            lam = (jnp.exp(jnp.sum(lv[0:1] * lv[1:2], axis=1, keepdims=True))
                   - jnp.exp(jnp.sum(lv[2:3] * lv[3:4], axis=1, keepdims=True)) + lambda_init)
            o = (o0 - lam * o1).T
            o = o * lax.rsqrt(jnp.mean(o * o, axis=-1, keepdims=True) + EPS)
            o = o * g_ref[...] * (1.0 - lambda_init)
        o_ref[0, qi * t:(qi + 1) * t, :] = o.astype(o_ref.dtype)


def _attention(mode, qkv, n_groups, extra, lambda_init=0.0):
    b, s, _ = qkv.shape
    t = ATTN_TILE
    nt = s // t
    assert (nt * (nt - 1)) % 4 == 0
    tab = jnp.asarray(_attn_items(nt), jnp.int32)
    in_specs = [
        pl.BlockSpec(memory_space=pltpu.SMEM),
        pl.BlockSpec((1, s, LANES), lambda i, h: (i, 0, h)),
        pl.BlockSpec((1, s, LANES), lambda i, h: (i, 0, n_groups + h)),
        pl.BlockSpec((1, s, LANES), lambda i, h: (i, 0, 2 * n_groups + h)),
    ]
    if mode == "fox":
        (kx,) = extra
        in_specs.append(pl.BlockSpec((1, s, LANES), lambda i, h: (i, 0, 0)))
        dv, nv, kq = HEAD_DIM, 2, 2 * LANES
    else:
        lamv, g = extra
        in_specs += [_const_spec(lamv.shape), _const_spec(g.shape)]
        dv, nv, kq = LANES, 1, LANES
    return pl.pallas_call(
        functools.partial(_attn_kernel, mode=mode, lambda_init=lambda_init),
        grid=(b, n_groups),
        in_specs=in_specs,
        out_specs=pl.BlockSpec((1, s, LANES), lambda i, h: (i, 0, h)),
        out_shape=jax.ShapeDtypeStruct((b, s, n_groups * LANES), BF16),
        scratch_shapes=[
            pltpu.VMEM((nv, dv + ONES_ROWS, s), BF16),
            pltpu.VMEM((2, kq, s), BF16),
            pltpu.VMEM((t, t), F32),
            pltpu.VMEM((t, t), F32),
            pltpu.VMEM((t, t), F32),
            pltpu.VMEM((t, t), F32),
            pltpu.VMEM((2, 2, 1, t), F32),
            pltpu.VMEM((nt, 2, 1, t), F32),
            pltpu.VMEM((nt, 2, dv + ONES_ROWS, t), F32),
        ],
        compiler_params=_params("parallel", "parallel"),
        name=mode + "_attn",
    )(tab, qkv, qkv, qkv, *extra)


def _out_proj_kernel(*refs, n_in):
    h_ref = refs[0]
    ys = refs[1:1 + n_in]
    ws = refs[1 + n_in:1 + 2 * n_in]
    o_ref = refs[1 + 2 * n_in]
    acc = h_ref[...]
    for y_ref, w_ref in zip(ys, ws):
        acc = acc + _dot(y_ref[...], w_ref[...])
    o_ref[...] = acc


def _out_proj(h2d, ys, ws):
    m, d = h2d.shape
    tm = ROW_TILE
    n_in = len(ys)
    in_specs = [pl.BlockSpec((tm, d), lambda i: (i, 0))]
    in_specs += [pl.BlockSpec((tm, y.shape[1]), lambda i: (i, 0)) for y in ys]
    in_specs += [_const_spec(w.shape) for w in ws]
    return pl.pallas_call(
        functools.partial(_out_proj_kernel, n_in=n_in),
        grid=(m // tm,),
        in_specs=in_specs,
        out_specs=pl.BlockSpec((tm, d), lambda i: (i, 0)),
        out_shape=jax.ShapeDtypeStruct((m, d), F32),
        compiler_params=_params("parallel"),
        name="out_proj",
    )(h2d, *ys, *ws)


def _xattn_kernel(h_ref, g_ref, wq_ref, k_ref, v_ref, wo_ref, o_ref):
    d = h_ref.shape[2]
    dh = d // XATTN_HEADS
    h = h_ref[0]
    q = _dot(_rms(h, g_ref[...]).astype(BF16), wq_ref[...]).astype(BF16)
    outs = []
    for e in range(XATTN_HEADS):
        sl = slice(e * dh, (e + 1) * dh)
        s = _dot_nt(q[:, sl], k_ref[0, :, sl]) * (dh ** -0.5)
        p = jnp.exp(s - jnp.max(s, axis=1, keepdims=True))
        p = p / jnp.sum(p, axis=1, keepdims=True)
        outs.append(_dot(p.astype(BF16), v_ref[0, :, sl]).astype(BF16))
    o = jnp.concatenate(outs, axis=1)
    o_ref[0] = h + _dot(o, wo_ref[...])


def _xattn(h, g, wq, kv, wo):
    b, s, d = h.shape
    tm = ROW_TILE
    return pl.pallas_call(
        _xattn_kernel,
        grid=(b, s // tm),
        in_specs=[
            pl.BlockSpec((1, tm, d), lambda i, j: (i, j, 0)),
            _const_spec((1, d)),
            _const_spec((d, d)),
            pl.BlockSpec((1, N_MEM, d), lambda i, j: (i, 0, 0)),
            pl.BlockSpec((1, N_MEM, d), lambda i, j: (i, 0, 1)),
            _const_spec((d, d)),
        ],
        out_specs=pl.BlockSpec((1, tm, d), lambda i, j: (i, j, 0)),
        out_shape=jax.ShapeDtypeStruct((b, s, d), F32),
        compiler_params=_params("parallel", "parallel"),
        name="xattn",
    )(h, g, wq, kv, kv, wo)


def _mlp_kernel(h_ref, g_ref, wu_ref, wd_ref, gf_ref, o_ref, *, final_norm, ff_chunk):
    h = h_ref[...]
    xn = _rms(h, g_ref[...]).astype(BF16)
    acc = h
    for c in range(wu_ref.shape[1] // ff_chunk):
        sl = slice(c * ff_chunk, (c + 1) * ff_chunk)
        u = jnp.maximum(_dot(xn, wu_ref[:, sl]), 0.0)
        acc = acc + _dot((u * u).astype(BF16), wd_ref[sl, :])
    if final_norm:
        acc = _rms(acc, gf_ref[...])
    o_ref[...] = acc


def _mlp(h2d, g, wu, wd, gf, final_norm):
    m, d = h2d.shape
    tm = ROW_TILE
    return pl.pallas_call(
        functools.partial(_mlp_kernel, final_norm=final_norm, ff_chunk=1024),
        grid=(m // tm,),
        in_specs=[
            pl.BlockSpec((tm, d), lambda i: (i, 0)),
            _const_spec((1, d)),
            _const_spec(wu.shape),
            _const_spec(wd.shape),
            _const_spec((1, d)),
        ],
        out_specs=pl.BlockSpec((tm, d), lambda i: (i, 0)),
        out_shape=jax.ShapeDtypeStruct((m, d), F32),
        compiler_params=_params("parallel"),
        name="mlp",
    )(h2d, g, wu, wd, gf)


def _block_diag(w):
    g, n, _ = w.shape
    eye = jnp.eye(g, dtype=w.dtype)
    return (eye[:, None, :, None] * w[:, :, None, :]).reshape(g * n, g * n)


def kernel(x, mem, g_mem, g_final, mix_norm_g, xattn_norm_g, mlp_norm_g, w_in_even, conv_w, conv_b, w_rgate, b_rgate, w_igate, b_igate, lru_lambda, fox_forget_b, w_out_even, w_in_odd, lambda_q1, lambda_k1, lambda_q2, lambda_k2, diff_norm_g, w_out_odd, xattn_wq, xattn_wkv, xattn_wo, w_up, w_down):
    b, s, d = x.shape
    depth = mix_norm_g.shape[0]
    m = b * s
    row = lambda v: v.reshape(1, -1).astype(F32)

    h = x
    for layer in range(depth):
        i = layer // 2
        if layer % 2 == 0:
            w_in = jnp.pad(w_in_even[i], ((0, 0), (0, 2688 - w_in_even.shape[2]))).astype(BF16)
            zl, qkv, kx = _even_proj(h, row(mix_norm_g[layer]), w_in,
                                     fox_forget_b[i].reshape(FOX_HEADS, 1).astype(F32))
            wg = jnp.concatenate([_block_diag(w_rgate[i]), _block_diag(w_igate[i])], axis=1).astype(BF16)
            bg = jnp.concatenate([b_rgate[i], b_igate[i]]).reshape(1, -1).astype(F32)
            y_lru = _lru(zl, conv_w[i].astype(F32), row(conv_b[i]), wg, bg, row(lru_lambda[i]))
            y_fox = _attention("fox", qkv, FOX_HEADS // 2, (kx,))
            w_out = w_out_even[i].astype(BF16)
            h2 = _out_proj(h.reshape(m, d), [y_lru.reshape(m, -1), y_fox.reshape(m, -1)],
                           [w_out[:LRU_WIDTH], w_out[LRU_WIDTH:]])
        else:
            z = _norm_proj(h.reshape(m, d), row(mix_norm_g[layer]), w_in_odd[i].astype(BF16), BF16,
                           q_cols=DIFF_HEADS * 2 * HEAD_DIM)
            lambda_init = 0.8 - 0.6 * math.exp(-0.3 * layer)
            lamv = jnp.stack([lambda_q1[i], lambda_k1[i], lambda_q2[i], lambda_k2[i]]).astype(F32)
            y = _attention("diff", z.reshape(b, s, -1), DIFF_HEADS, (lamv, row(diff_norm_g[i])),
                           lambda_init=lambda_init)
            h2 = _out_proj(h.reshape(m, d), [y.reshape(m, -1)], [w_out_odd[i].astype(BF16)])

        kv = _norm_proj(mem.reshape(b * N_MEM, d), row(g_mem), xattn_wkv[layer].astype(BF16), BF16)
        h3 = _xattn(h2.reshape(b, s, d), row(xattn_norm_g[layer]), xattn_wq[layer].astype(BF16),
                    kv.reshape(b, N_MEM, 2 * d), xattn_wo[layer].astype(BF16))
        h4 = _mlp(h3.reshape(m, d), row(mlp_norm_g[layer]), w_up[layer].astype(BF16),
                  w_down[layer].astype(BF16), row(g_final), final_norm=(layer == depth - 1))
        h = h4.reshape(b, s, d)
    return h
```

```python
import functools
import math

import jax
import jax.numpy as jnp
from jax import lax
from jax.experimental import pallas as pl
from jax.experimental.pallas import tpu as pltpu

F32 = jnp.float32
BF16 = jnp.bfloat16

LANES = 128
SUBLANES = 8
VMEM_LIMIT = 56 * 1024 * 1024

EPS = 1e-6
N_MEM = 256
LRU_WIDTH = 512
LRU_BLOCKS = 8
CONV_WIDTH = 4
LRU_C = 8.0
HEAD_DIM = 64
FOX_HEADS = 8
DIFF_HEADS = 8
XATTN_HEADS = 4

LOG2E = 1.4426950408889634
QK_SCALE = HEAD_DIM ** -0.5 * LOG2E
ONES_ROWS = 16

ROW_TILE = 512
ATTN_TILE = 512
LRU_TILE = 256


def _params(*sem):
    return pltpu.CompilerParams(dimension_semantics=sem, vmem_limit_bytes=VMEM_LIMIT)


def _rms(x, g):
    return x * lax.rsqrt(jnp.mean(x * x, axis=-1, keepdims=True) + EPS) * g


def _dot(a, b):
    return jnp.dot(a, b, preferred_element_type=F32)


def _dot_nt(a, b):
    return lax.dot_general(a, b, (((1,), (1,)), ((), ())), preferred_element_type=F32)


def _log_sigmoid(x):
    return jnp.minimum(x, 0.0) - jnp.log1p(jnp.exp(-jnp.abs(x)))


def _sigmoid(x):
    return 0.5 * jnp.tanh(0.5 * x) + 0.5


def _expm1_given_exp(x, u):
    near = jnp.where(u == 1.0, x, (u - 1.0) * x / jnp.log(u))
    return jnp.where(jnp.abs(x) < 0.5, near, u - 1.0)


def _const_spec(shape):
    nd = len(shape)
    return pl.BlockSpec(shape, lambda *_: (0,) * nd)


def _even_proj_kernel(x_ref, g_ref, w_ref, fb_ref, zl_ref, qkv_ref, kx_ref, carry_ref):
    si = pl.program_id(1)
    tm = x_ref.shape[1]
    xn = _rms(x_ref[0], g_ref[...]).astype(BF16)
    zl_ref[0] = _dot(xn, w_ref[:, 0:1024])
    qkv_ref[0, :, 0:512] = (_dot(xn, w_ref[:, 1024:1536]) * QK_SCALE).astype(BF16)
    qkv_ref[0, :, 512:1536] = _dot(xn, w_ref[:, 1536:2560]).astype(BF16)
    fl = _dot(xn, w_ref[:, 2560:2688])
    lf = _log_sigmoid(fl.T[0:FOX_HEADS, :] + fb_ref[...])

    @pl.when(si == 0)
    def _():
        carry_ref[...] = jnp.zeros_like(carry_ref)

    pos = lax.broadcasted_iota(jnp.int32, lf.shape, 1)
    shift = 1
    while shift < tm:
        lf = lf + jnp.where(pos >= shift, pltpu.roll(lf, shift, 1), 0.0)
        shift *= 2
    cum = lf + carry_ref[:, 0:1]
    carry_ref[...] = jnp.broadcast_to(cum[:, tm - 1:tm], carry_ref.shape)

    c = cum * (-LOG2E)
    hi = c.astype(BF16).astype(F32)
    mid = (c - hi).astype(BF16).astype(F32)
    lo = (c - hi - mid).astype(BF16).astype(F32)
    pieces = jnp.concatenate([hi, mid, lo, jnp.zeros((LANES - 3 * FOX_HEADS, tm), F32)], axis=0)
    kx_ref[0] = pieces.T.astype(BF16)


def _even_proj(h, g, w, fb):
    b, s, d = h.shape
    tm = ROW_TILE
    n = w.shape[1]
    return pl.pallas_call(
        _even_proj_kernel,
        grid=(b, s // tm),
        in_specs=[
            pl.BlockSpec((1, tm, d), lambda i, j: (i, j, 0)),
            _const_spec((1, d)),
            _const_spec((d, n)),
            _const_spec((FOX_HEADS, 1)),
        ],
        out_specs=[
            pl.BlockSpec((1, tm, 1024), lambda i, j: (i, j, 0)),
            pl.BlockSpec((1, tm, 1536), lambda i, j: (i, j, 0)),
            pl.BlockSpec((1, tm, LANES), lambda i, j: (i, j, 0)),
        ],
        out_shape=[
            jax.ShapeDtypeStruct((b, s, 1024), F32),
            jax.ShapeDtypeStruct((b, s, 1536), BF16),
            jax.ShapeDtypeStruct((b, s, LANES), BF16),
        ],
        scratch_shapes=[pltpu.VMEM((FOX_HEADS, LANES), F32)],
        compiler_params=_params("parallel", "arbitrary"),
        name="even_proj",
    )(h, g, w, fb)


def _norm_proj_kernel(x_ref, g_ref, w_ref, o_ref, *, q_cols):
    xn = _rms(x_ref[...], g_ref[...]).astype(BF16)
    if q_cols:
        o_ref[:, 0:q_cols] = (_dot(xn, w_ref[:, 0:q_cols]) * QK_SCALE).astype(o_ref.dtype)
    o_ref[:, q_cols:] = _dot(xn, w_ref[:, q_cols:]).astype(o_ref.dtype)


def _norm_proj(x2d, g, w, out_dtype, q_cols=0):
    m, d = x2d.shape
    n = w.shape[1]
    tm = min(ROW_TILE, m)
    return pl.pallas_call(
        functools.partial(_norm_proj_kernel, q_cols=q_cols),
        grid=(m // tm,),
        in_specs=[pl.BlockSpec((tm, d), lambda i: (i, 0)), _const_spec((1, d)), _const_spec((d, n))],
        out_specs=pl.BlockSpec((tm, n), lambda i: (i, 0)),
        out_shape=jax.ShapeDtypeStruct((m, n), out_dtype),
        compiler_params=_params("parallel"),
        name="norm_proj",
    )(x2d, g, w)


def _gelu_tanh(x):
    return 0.5 * x * (1.0 + jnp.tanh(math.sqrt(2.0 / math.pi) * (x + 0.044715 * (x * x * x))))


def _lru_kernel(xb_ref, gb_ref, cw_ref, cb_ref, wg_ref, bg_ref, lam_ref, y_ref,
                xbuf, a_s, u_s, hc):
    ti = pl.program_id(1)
    ts = xb_ref.shape[1]
    w = LRU_WIDTH

    @pl.when(ti == 0)
    def _():
        xbuf[0:SUBLANES, :] = jnp.zeros((SUBLANES, w), F32)
        hc[...] = jnp.zeros_like(hc)

    xbuf[SUBLANES:SUBLANES + ts, :] = xb_ref[0]
    xc = cb_ref[...]
    for k in range(CONV_WIDTH):
        lag = CONV_WIDTH - 1 - k
        xc = xc + xbuf[SUBLANES - lag:SUBLANES - lag + ts, :] * cw_ref[k:k + 1, :]
    xbuf[0:SUBLANES, :] = xbuf[ts:ts + SUBLANES, :]

    gates = _dot(xc.astype(BF16), wg_ref[...]) + bg_ref[...]
    r = _sigmoid(gates[:, 0:w])
    i = _sigmoid(gates[:, w:2 * w])
    log_a = LRU_C * r * _log_sigmoid(lam_ref[...])
    a = jnp.exp(log_a)
    a_s[...] = a
    u_s[...] = jnp.sqrt(-_expm1_given_exp(2.0 * log_a, a * a)) * (i * xc)

    row = lax.broadcasted_iota(jnp.int32, (SUBLANES, w), 0)

    def group(gidx, carry):
        off = pl.multiple_of(gidx * SUBLANES, SUBLANES)
        a = a_s[pl.ds(off, SUBLANES), :]
        u = u_s[pl.ds(off, SUBLANES), :]
        for sh in (1, 2, 4):
            keep = row >= sh
            u = u + a * jnp.where(keep, pltpu.roll(u, sh, 0), 0.0)
            a = a * jnp.where(keep, pltpu.roll(a, sh, 0), 1.0)
        hgrp = u + a * carry
        u_s[pl.ds(off, SUBLANES), :] = hgrp
        return jnp.broadcast_to(hgrp[SUBLANES - 1:SUBLANES, :], (SUBLANES, w))

    hc[...] = lax.fori_loop(0, ts // SUBLANES, group, hc[...], unroll=4)
    y_ref[0] = (u_s[...] * _gelu_tanh(gb_ref[0])).astype(y_ref.dtype)


def _lru(zl, cw, cb, wg, bg, lam):
    b, s, _ = zl.shape
    ts = LRU_TILE
    w = LRU_WIDTH
    return pl.pallas_call(
        _lru_kernel,
        grid=(b, s // ts),
        in_specs=[
            pl.BlockSpec((1, ts, w), lambda i, j: (i, j, 0)),
            pl.BlockSpec((1, ts, w), lambda i, j: (i, j, 1)),
            _const_spec((CONV_WIDTH, w)),
            _const_spec((1, w)),
            _const_spec((w, 2 * w)),
            _const_spec((1, 2 * w)),
            _const_spec((1, w)),
        ],
        out_specs=pl.BlockSpec((1, ts, w), lambda i, j: (i, j, 0)),
        out_shape=jax.ShapeDtypeStruct((b, s, w), BF16),
        scratch_shapes=[
            pltpu.VMEM((SUBLANES + ts, w), F32),
            pltpu.VMEM((ts, w), F32),
            pltpu.VMEM((ts, w), F32),
            pltpu.VMEM((SUBLANES, w), F32),
        ],
        compiler_params=_params("parallel", "arbitrary"),
        name="rglru",
    )(zl, zl, cw, cb, wg, bg, lam)


def _attn_items(nt):
    items = [(qi, j) for qi in range(1, nt) for j in range(qi)]
    return items + [items[-1]] * 2


def _attn_kernel(*refs, mode, lambda_init):
    if mode == "fox":
        tab_ref, q_ref, k_ref, v_ref, kx_ref, o_ref, vt_s, qt_s, sa0, sa1, sb0, sb1, mc_s, m_s, acc_s = refs
    else:
        tab_ref, q_ref, k_ref, v_ref, lamv_ref, g_ref, o_ref, vt_s, qt_s, sa0, sa1, sb0, sb1, mc_s, m_s, acc_s = refs
    bufs = ((sa0, sa1), (sb0, sb1))
    hp = pl.program_id(1)
    t = ATTN_TILE
    s_len = k_ref.shape[1]
    nt = s_len // t
    dv = HEAD_DIM if mode == "fox" else LANES

    rowi = lax.broadcasted_iota(jnp.int32, (LANES, t), 0)
    for c in range(nt):
        cols = slice(c * t, (c + 1) * t)
        vt = v_ref[0, cols, :].astype(F32).T.astype(BF16)
        ones = jnp.ones((ONES_ROWS, t), BF16)
        if mode == "fox":
            for e in range(2):
                vt_s[e, 0:dv, cols] = vt[e * dv:(e + 1) * dv]
                vt_s[e, dv:dv + ONES_ROWS, cols] = ones
        else:
            vt_s[0, 0:dv, cols] = vt
            vt_s[0, dv:dv + ONES_ROWS, cols] = ones
        qt = q_ref[0, cols, :].astype(F32).T.astype(BF16)
        zero = jnp.zeros_like(qt)
        for e in range(2):
            keep = (rowi < HEAD_DIM) if e == 0 else (rowi >= HEAD_DIM)
            qt_s[e, 0:LANES, cols] = jnp.where(keep, qt, zero)
            if mode == "fox":
                head = 2 * hp + e
                pick = (rowi == head) | (rowi == head + FOX_HEADS) | (rowi == head + 2 * FOX_HEADS)
                qt_s[e, LANES:2 * LANES, cols] = jnp.where(pick, 1.0, 0.0).astype(BF16)

    m_s[...] = jnp.full(m_s.shape, -jnp.inf, F32)
    acc_s[...] = jnp.zeros(acc_s.shape, F32)

    def colmax(x):
        acc = x[0:16]
        for i in range(1, x.shape[0] // 16):
            acc = jnp.maximum(acc, x[i * 16:(i + 1) * 16])
        return jnp.max(acc, axis=0, keepdims=True)

    def stage1(qi, j, par, e, masked):
        off = pl.multiple_of(j * t, t)
        kb = k_ref[0, pl.ds(off, t), :]
        if mode == "fox":
            kb = jnp.concatenate([kb, kx_ref[0, pl.ds(off, t), :]], axis=1)
        s = _dot(kb, qt_s[e, :, pl.ds(pl.multiple_of(qi * t, t), t)])
        if masked:
            keys = lax.broadcasted_iota(jnp.int32, (t, t), 0)
            qpos = lax.broadcasted_iota(jnp.int32, (t, t), 1)
            s = jnp.where(keys <= qpos, s, -jnp.inf)
        bufs[par][e][...] = s
        mc_s[par, e] = colmax(s)

    def stage2(qi, j, par, e):
        off = pl.multiple_of(j * t, t)
        vt = vt_s[e if mode == "fox" else 0, :, pl.ds(off, t)]
        m_prev = m_s[qi, e]
        m_new = jnp.maximum(m_prev, mc_s[par, e])
        alpha = jnp.exp2(m_prev - m_new)
        p = jnp.exp2(bufs[par][e][...] - m_new).astype(BF16)
        acc_s[qi, e] = alpha * acc_s[qi, e] + _dot(vt, p)
        m_s[qi, e] = m_new

    def item(w):
        return tab_ref[w, 0], tab_ref[w, 1]

    qa, ja = item(0)
    stage1(qa, ja, 0, 0, False)
    stage1(qa, ja, 0, 1, False)

    def body(i, c):
        w = 2 * i
        for par in range(2):
            qc, jc = item(w + par)
            qn, jn = item(w + par + 1)
            for e in range(2):
                stage1(qn, jn, 1 - par, e, False)
                stage2(qc, jc, par, e)
        return c

    lax.fori_loop(0, (nt * (nt - 1)) // 4, body, 0)

    stage1(0, 0, 0, 0, True)
    stage1(0, 0, 0, 1, True)
    for qi in range(nt):
        par = qi % 2
        for e in range(2):
            if qi + 1 < nt:
                stage1(qi + 1, qi + 1, 1 - par, e, True)
            stage2(qi, qi, par, e)

    for qi in range(nt):
        o0 = acc_s[qi, 0, 0:dv] / acc_s[qi, 0, dv:dv + 1]
        o1 = acc_s[qi, 1, 0:dv] / acc_s[qi, 1, dv:dv + 1]
        if mode == "fox":
            o = jnp.concatenate([o0, o1], axis=0).T
        else:
            lv = lamv_ref[...]
            lam = (jnp.exp(jnp.sum(lv[0:1] * lv[1:2], axis=1, keepdims=True))
                   - jnp.exp(jnp.sum(lv[2:3] * lv[3:4], axis=1, keepdims=True)) + lambda_init)
            o = (o0 - lam * o1).T
            o = o * lax.rsqrt(jnp.mean(o * o, axis=-1, keepdims=True) + EPS)
            o = o * g_ref[...] * (1.0 - lambda_init)
        o_ref[0, qi * t:(qi + 1) * t, :] = o.astype(o_ref.dtype)


def _attention(mode, qkv, n_groups, extra, lambda_init=0.0):
    b, s, _ = qkv.shape
    t = ATTN_TILE
    nt = s // t
    assert (nt * (nt - 1)) % 4 == 0
    tab = jnp.asarray(_attn_items(nt), jnp.int32)
    in_specs = [
        pl.BlockSpec(memory_space=pltpu.SMEM),
        pl.BlockSpec((1, s, LANES), lambda i, h: (i, 0, h)),
        pl.BlockSpec((1, s, LANES), lambda i, h: (i, 0, n_groups + h)),
        pl.BlockSpec((1, s, LANES), lambda i, h: (i, 0, 2 * n_groups + h)),
    ]
    if mode == "fox":
        (kx,) = extra
        in_specs.append(pl.BlockSpec((1, s, LANES), lambda i, h: (i, 0, 0)))
        dv, nv, kq = HEAD_DIM, 2, 2 * LANES
    else:
        lamv, g = extra
        in_specs += [_const_spec(lamv.shape), _const_spec(g.shape)]
        dv, nv, kq = LANES, 1, LANES
    return pl.pallas_call(
        functools.partial(_attn_kernel, mode=mode, lambda_init=lambda_init),
        grid=(b, n_groups),
        in_specs=in_specs,
        out_specs=pl.BlockSpec((1, s, LANES), lambda i, h: (i, 0, h)),
        out_shape=jax.ShapeDtypeStruct((b, s, n_groups * LANES), BF16),
        scratch_shapes=[
            pltpu.VMEM((nv, dv + ONES_ROWS, s), BF16),
            pltpu.VMEM((2, kq, s), BF16),
            pltpu.VMEM((t, t), F32),
            pltpu.VMEM((t, t), F32),
            pltpu.VMEM((t, t), F32),
            pltpu.VMEM((t, t), F32),
            pltpu.VMEM((2, 2, 1, t), F32),
            pltpu.VMEM((nt, 2, 1, t), F32),
            pltpu.VMEM((nt, 2, dv + ONES_ROWS, t), F32),
        ],
        compiler_params=_params("parallel", "parallel"),
        name=mode + "_attn",
    )(tab, qkv, qkv, qkv, *extra)


FF_CHUNK = 1024


def _post_kernel(*refs, n_mix, final_norm):
    h_ref = refs[0]
    ys = refs[1:1 + n_mix]
    ws = refs[1 + n_mix:1 + 2 * n_mix]
    gx_ref, wq_ref, k_ref, v_ref, wo_ref, gm_ref, wu_ref, wd_ref, gf_ref, o_ref = refs[1 + 2 * n_mix:]
    d = h_ref.shape[2]
    dh = d // XATTN_HEADS

    h = h_ref[0]
    for y_ref, w_ref in zip(ys, ws):
        h = h + _dot(y_ref[0], w_ref[...])

    q = _dot(_rms(h, gx_ref[...]).astype(BF16), wq_ref[...]).astype(BF16)
    outs = []
    for e in range(XATTN_HEADS):
        sl = slice(e * dh, (e + 1) * dh)
        s = _dot_nt(q[:, sl], k_ref[0, :, sl]) * (dh ** -0.5)
        p = jnp.exp(s - jnp.max(s, axis=1, keepdims=True))
        p = p / jnp.sum(p, axis=1, keepdims=True)
        outs.append(_dot(p.astype(BF16), v_ref[0, :, sl]).astype(BF16))
    h = h + _dot(jnp.concatenate(outs, axis=1), wo_ref[...])

    xn = _rms(h, gm_ref[...]).astype(BF16)
    for c in range(wu_ref.shape[1] // FF_CHUNK):
        sl = slice(c * FF_CHUNK, (c + 1) * FF_CHUNK)
        u = jnp.maximum(_dot(xn, wu_ref[:, sl]), 0.0)
        h = h + _dot((u * u).astype(BF16), wd_ref[sl, :])
    if final_norm:
        h = _rms(h, gf_ref[...])
    o_ref[0] = h


def _post(h, ys, ws, gx, wq, kv, wo, gm, wu, wd, gf, final_norm):
    b, s, d = h.shape
    tm = ROW_TILE
    n_mix = len(ys)
    tile = lambda width: pl.BlockSpec((1, tm, width), lambda i, j: (i, j, 0))
    in_specs = [tile(d)] + [tile(y.shape[2]) for y in ys] + [_const_spec(w.shape) for w in ws]
    in_specs += [
        _const_spec((1, d)),
        _const_spec(wq.shape),
        pl.BlockSpec((1, N_MEM, d), lambda i, j: (i, 0, 0)),
        pl.BlockSpec((1, N_MEM, d), lambda i, j: (i, 0, 1)),
        _const_spec(wo.shape),
        _const_spec((1, d)),
        _const_spec(wu.shape),
        _const_spec(wd.shape),
        _const_spec((1, d)),
    ]
    return pl.pallas_call(
        functools.partial(_post_kernel, n_mix=n_mix, final_norm=final_norm),
        grid=(b, s // tm),
        in_specs=in_specs,
        out_specs=tile(d),
        out_shape=jax.ShapeDtypeStruct((b, s, d), F32),
        compiler_params=_params("parallel", "parallel"),
        name="post",
    )(h, *ys, *ws, gx, wq, kv, kv, wo, gm, wu, wd, gf)


def _block_diag(w):
    g, n, _ = w.shape
    eye = jnp.eye(g, dtype=w.dtype)
    return (eye[:, None, :, None] * w[:, :, None, :]).reshape(g * n, g * n)


def kernel(x, mem, g_mem, g_final, mix_norm_g, xattn_norm_g, mlp_norm_g, w_in_even, conv_w, conv_b, w_rgate, b_rgate, w_igate, b_igate, lru_lambda, fox_forget_b, w_out_even, w_in_odd, lambda_q1, lambda_k1, lambda_q2, lambda_k2, diff_norm_g, w_out_odd, xattn_wq, xattn_wkv, xattn_wo, w_up, w_down):
    b, s, d = x.shape
    depth = mix_norm_g.shape[0]
    m = b * s
    row = lambda v: v.reshape(1, -1).astype(F32)

    h = x
    for layer in range(depth):
        i = layer // 2
        if layer % 2 == 0:
            w_in = jnp.pad(w_in_even[i], ((0, 0), (0, 2688 - w_in_even.shape[2]))).astype(BF16)
            zl, qkv, kx = _even_proj(h, row(mix_norm_g[layer]), w_in,
                                     fox_forget_b[i].reshape(FOX_HEADS, 1).astype(F32))
            wg = jnp.concatenate([_block_diag(w_rgate[i]), _block_diag(w_igate[i])], axis=1).astype(BF16)
            bg = jnp.concatenate([b_rgate[i], b_igate[i]]).reshape(1, -1).astype(F32)
            y_lru = _lru(zl, conv_w[i].astype(F32), row(conv_b[i]), wg, bg, row(lru_lambda[i]))
            y_fox = _attention("fox", qkv, FOX_HEADS // 2, (kx,))
            w_out = w_out_even[i].astype(BF16)
            ys, ws = [y_lru, y_fox], [w_out[:LRU_WIDTH], w_out[LRU_WIDTH:]]
        else:
            z = _norm_proj(h.reshape(m, d), row(mix_norm_g[layer]), w_in_odd[i].astype(BF16), BF16,
                           q_cols=DIFF_HEADS * 2 * HEAD_DIM)
            lambda_init = 0.8 - 0.6 * math.exp(-0.3 * layer)
            lamv = jnp.stack([lambda_q1[i], lambda_k1[i], lambda_q2[i], lambda_k2[i]]).astype(F32)
            y = _attention("diff", z.reshape(b, s, -1), DIFF_HEADS, (lamv, row(diff_norm_g[i])),
                           lambda_init=lambda_init)
            ys, ws = [y], [w_out_odd[i].astype(BF16)]

        kv = _norm_proj(mem.reshape(b * N_MEM, d), row(g_mem), xattn_wkv[layer].astype(BF16), BF16)
        h = _post(h, ys, ws, row(xattn_norm_g[layer]), xattn_wq[layer].astype(BF16),
                  kv.reshape(b, N_MEM, 2 * d), xattn_wo[layer].astype(BF16),
                  row(mlp_norm_g[layer]), w_up[layer].astype(BF16), w_down[layer].astype(BF16),
                  row(g_final), final_norm=(layer == depth - 1))
    return h
```

```python
import functools
import math

import jax
import jax.numpy as jnp
from jax import lax
from jax.experimental import pallas as pl
from jax.experimental.pallas import tpu as pltpu

F32 = jnp.float32
BF16 = jnp.bfloat16

LANES = 128
SUBLANES = 8
VMEM_LIMIT = 56 * 1024 * 1024

EPS = 1e-6
N_MEM = 256
LRU_WIDTH = 512
LRU_BLOCKS = 8
CONV_WIDTH = 4
LRU_C = 8.0
HEAD_DIM = 64
FOX_HEADS = 8
DIFF_HEADS = 8
XATTN_HEADS = 4

LOG2E = 1.4426950408889634
QK_SCALE = HEAD_DIM ** -0.5 * LOG2E
ONES_ROWS = 16

ROW_TILE = 512
ATTN_TILE = 512
LRU_TILE = 256


def _params(*sem):
    return pltpu.CompilerParams(dimension_semantics=sem, vmem_limit_bytes=VMEM_LIMIT)


def _rms(x, g):
    return x * lax.rsqrt(jnp.mean(x * x, axis=-1, keepdims=True) + EPS) * g


def _dot(a, b):
    return jnp.dot(a, b, preferred_element_type=F32)


def _dot_nt(a, b):
    return lax.dot_general(a, b, (((1,), (1,)), ((), ())), preferred_element_type=F32)


def _log_sigmoid(x):
    return jnp.minimum(x, 0.0) - jnp.log1p(jnp.exp(-jnp.abs(x)))


def _sigmoid(x):
    return 0.5 * jnp.tanh(0.5 * x) + 0.5


def _expm1_given_exp(x, u):
    near = jnp.where(u == 1.0, x, (u - 1.0) * x / jnp.log(u))
    return jnp.where(jnp.abs(x) < 0.5, near, u - 1.0)


def _const_spec(shape):
    nd = len(shape)
    return pl.BlockSpec(shape, lambda *_: (0,) * nd)


def _even_proj_kernel(x_ref, g_ref, w_ref, fb_ref, zl_ref, qkv_ref, kx_ref, carry_ref):
    si = pl.program_id(1)
    tm = x_ref.shape[1]
    xn = _rms(x_ref[0], g_ref[...]).astype(BF16)
    fl = _dot(xn, w_ref[:, 2560:2688])
    lf = _log_sigmoid(fl.T[0:FOX_HEADS, :] + fb_ref[...])

    @pl.when(si == 0)
    def _():
        carry_ref[...] = jnp.zeros_like(carry_ref)

    pos = lax.broadcasted_iota(jnp.int32, lf.shape, 1)
    shift = 1
    while shift < tm:
        lf = lf + jnp.where(pos >= shift, pltpu.roll(lf, shift, 1), 0.0)
        shift *= 2
    cum = lf + carry_ref[:, 0:1]
    carry_ref[...] = jnp.broadcast_to(cum[:, tm - 1:tm], carry_ref.shape)

    c = cum * (-LOG2E)
    hi = c.astype(BF16).astype(F32)
    mid = (c - hi).astype(BF16).astype(F32)
    lo = (c - hi - mid).astype(BF16).astype(F32)
    pieces = jnp.concatenate([hi, mid, lo, jnp.zeros((LANES - 3 * FOX_HEADS, tm), F32)], axis=0)
    kx_ref[0] = pieces.T.astype(BF16)

    zl_ref[0] = _dot(xn, w_ref[:, 0:1024])
    qkv_ref[0, :, 0:512] = (_dot(xn, w_ref[:, 1024:1536]) * QK_SCALE).astype(BF16)
    qkv_ref[0, :, 512:1536] = _dot(xn, w_ref[:, 1536:2560]).astype(BF16)


def _even_proj(h, g, w, fb):
    b, s, d = h.shape
    tm = ROW_TILE
    n = w.shape[1]
    return pl.pallas_call(
        _even_proj_kernel,
        grid=(b, s // tm),
        in_specs=[
            pl.BlockSpec((1, tm, d), lambda i, j: (i, j, 0)),
            _const_spec((1, d)),
            _const_spec((d, n)),
            _const_spec((FOX_HEADS, 1)),
        ],
        out_specs=[
            pl.BlockSpec((1, tm, 1024), lambda i, j: (i, j, 0)),
            pl.BlockSpec((1, tm, 1536), lambda i, j: (i, j, 0)),
            pl.BlockSpec((1, tm, LANES), lambda i, j: (i, j, 0)),
        ],
        out_shape=[
            jax.ShapeDtypeStruct((b, s, 1024), F32),
            jax.ShapeDtypeStruct((b, s, 1536), BF16),
            jax.ShapeDtypeStruct((b, s, LANES), BF16),
        ],
        scratch_shapes=[pltpu.VMEM((FOX_HEADS, LANES), F32)],
        compiler_params=_params("parallel", "arbitrary"),
        name="even_proj",
    )(h, g, w, fb)


def _norm_proj_kernel(x_ref, g_ref, w_ref, o_ref, *, q_cols):
    xn = _rms(x_ref[...], g_ref[...]).astype(BF16)
    if q_cols:
        o_ref[:, 0:q_cols] = (_dot(xn, w_ref[:, 0:q_cols]) * QK_SCALE).astype(o_ref.dtype)
    o_ref[:, q_cols:] = _dot(xn, w_ref[:, q_cols:]).astype(o_ref.dtype)


def _norm_proj(x2d, g, w, out_dtype, q_cols=0):
    m, d = x2d.shape
    n = w.shape[1]
    tm = min(ROW_TILE, m)
    return pl.pallas_call(
        functools.partial(_norm_proj_kernel, q_cols=q_cols),
        grid=(m // tm,),
        in_specs=[pl.BlockSpec((tm, d), lambda i: (i, 0)), _const_spec((1, d)), _const_spec((d, n))],
        out_specs=pl.BlockSpec((tm, n), lambda i: (i, 0)),
        out_shape=jax.ShapeDtypeStruct((m, n), out_dtype),
        compiler_params=_params("parallel"),
        name="norm_proj",
    )(x2d, g, w)


def _gelu_tanh(x):
    return 0.5 * x * (1.0 + jnp.tanh(math.sqrt(2.0 / math.pi) * (x + 0.044715 * (x * x * x))))


def _lru_kernel(xb_ref, gb_ref, cw_ref, cb_ref, wg_ref, bg_ref, lam_ref, y_ref,
                xbuf, a_s, u_s, hc):
    ti = pl.program_id(1)
    ts = xb_ref.shape[1]
    w = LRU_WIDTH

    @pl.when(ti == 0)
    def _():
        xbuf[0:SUBLANES, :] = jnp.zeros((SUBLANES, w), F32)
        hc[...] = jnp.zeros_like(hc)

    xbuf[SUBLANES:SUBLANES + ts, :] = xb_ref[0]
    xc = cb_ref[...]
    for k in range(CONV_WIDTH):
        lag = CONV_WIDTH - 1 - k
        xc = xc + xbuf[SUBLANES - lag:SUBLANES - lag + ts, :] * cw_ref[k:k + 1, :]
    xbuf[0:SUBLANES, :] = xbuf[ts:ts + SUBLANES, :]

    gates = _dot(xc.astype(BF16), wg_ref[...]) + bg_ref[...]
    r = _sigmoid(gates[:, 0:w])
    i = _sigmoid(gates[:, w:2 * w])
    log_a = LRU_C * r * _log_sigmoid(lam_ref[...])
    a = jnp.exp(log_a)
    a_s[...] = a
    u_s[...] = jnp.sqrt(-_expm1_given_exp(2.0 * log_a, a * a)) * (i * xc)

    row = lax.broadcasted_iota(jnp.int32, (SUBLANES, w), 0)

    def group(gidx, carry):
        off = pl.multiple_of(gidx * SUBLANES, SUBLANES)
        a = a_s[pl.ds(off, SUBLANES), :]
        u = u_s[pl.ds(off, SUBLANES), :]
        for sh in (1, 2, 4):
            keep = row >= sh
            u = u + a * jnp.where(keep, pltpu.roll(u, sh, 0), 0.0)
            a = a * jnp.where(keep, pltpu.roll(a, sh, 0), 1.0)
        hgrp = u + a * carry
        u_s[pl.ds(off, SUBLANES), :] = hgrp
        return jnp.broadcast_to(hgrp[SUBLANES - 1:SUBLANES, :], (SUBLANES, w))

    hc[...] = lax.fori_loop(0, ts // SUBLANES, group, hc[...], unroll=4)
    y_ref[0] = (u_s[...] * _gelu_tanh(gb_ref[0])).astype(y_ref.dtype)


def _lru(zl, cw, cb, wg, bg, lam):
    b, s, _ = zl.shape
    ts = LRU_TILE
    w = LRU_WIDTH
    return pl.pallas_call(
        _lru_kernel,
        grid=(b, s // ts),
        in_specs=[
            pl.BlockSpec((1, ts, w), lambda i, j: (i, j, 0)),
            pl.BlockSpec((1, ts, w), lambda i, j: (i, j, 1)),
            _const_spec((CONV_WIDTH, w)),
            _const_spec((1, w)),
            _const_spec((w, 2 * w)),
            _const_spec((1, 2 * w)),
            _const_spec((1, w)),
        ],
        out_specs=pl.BlockSpec((1, ts, w), lambda i, j: (i, j, 0)),
        out_shape=jax.ShapeDtypeStruct((b, s, w), BF16),
        scratch_shapes=[
            pltpu.VMEM((SUBLANES + ts, w), F32),
            pltpu.VMEM((ts, w), F32),
            pltpu.VMEM((ts, w), F32),
            pltpu.VMEM((SUBLANES, w), F32),
        ],
        compiler_params=_params("parallel", "arbitrary"),
        name="rglru",
    )(zl, zl, cw, cb, wg, bg, lam)


def _attn_items(nt):
    items = [(qi, j) for qi in range(1, nt) for j in range(qi)]
    return items + [items[-1]] * 2


def _attn_kernel(*refs, mode, lambda_init):
    if mode == "fox":
        tab_ref, q_ref, k_ref, v_ref, kx_ref, o_ref, vt_s, qt_s, pk_s, sa0, sa1, sb0, sb1, mc_s, m_s, acc_s = refs
    else:
        tab_ref, q_ref, k_ref, v_ref, lamv_ref, g_ref, o_ref, vt_s, qt_s, sa0, sa1, sb0, sb1, mc_s, m_s, acc_s = refs
    bufs = ((sa0, sa1), (sb0, sb1))
    hp = pl.program_id(1)
    t = ATTN_TILE
    s_len = k_ref.shape[1]
    nt = s_len // t
    dv = HEAD_DIM if mode == "fox" else LANES

    for c in range(nt):
        cols = slice(c * t, (c + 1) * t)
        vt = v_ref[0, cols, :].astype(F32).T.astype(BF16)
        ones = jnp.ones((ONES_ROWS, t), BF16)
        if mode == "fox":
            for e in range(2):
                vt_s[e, 0:dv, cols] = vt[e * dv:(e + 1) * dv]
                vt_s[e, dv:dv + ONES_ROWS, cols] = ones
        else:
            vt_s[0, 0:dv, cols] = vt
            vt_s[0, dv:dv + ONES_ROWS, cols] = ones
        qt_s[:, cols] = q_ref[0, cols, :].astype(F32).T.astype(BF16)
    if mode == "fox":
        rowi = lax.broadcasted_iota(jnp.int32, (LANES, t), 0)
        for e in range(2):
            head = 2 * hp + e
            pick = (rowi == head) | (rowi == head + FOX_HEADS) | (rowi == head + 2 * FOX_HEADS)
            pk_s[e] = jnp.where(pick, 1.0, 0.0).astype(BF16)

    def q_weights(qi, e):
        qt = qt_s[:, pl.ds(pl.multiple_of(qi * t, t), t)]
        zero = jnp.zeros((HEAD_DIM, t), BF16)
        parts = [qt[0:HEAD_DIM], zero] if e == 0 else [zero, qt[HEAD_DIM:LANES]]
        if mode == "fox":
            parts.append(pk_s[e])
        return jnp.concatenate(parts, axis=0)

    def colmax(x):
        acc = x[0:16]
        for i in range(1, x.shape[0] // 16):
            acc = jnp.maximum(acc, x[i * 16:(i + 1) * 16])
        return jnp.max(acc, axis=0, keepdims=True)

    def key_block(off):
        kb = k_ref[0, pl.ds(off, t), :]
        if mode == "fox":
            kb = jnp.concatenate([kb, kx_ref[0, pl.ds(off, t), :]], axis=1)
        return kb

    def stage1(qi, j, par, e):
        s = _dot(key_block(pl.multiple_of(j * t, t)), q_weights(qi, e))
        bufs[par][e][...] = s
        mc_s[par, e] = colmax(s)

    def stage2(qi, j, par, e):
        off = pl.multiple_of(j * t, t)
        vt = vt_s[e if mode == "fox" else 0, :, pl.ds(off, t)]
        m_prev = m_s[qi, e]
        m_new = jnp.maximum(m_prev, mc_s[par, e])
        alpha = jnp.exp2(m_prev - m_new)
        p = jnp.exp2(bufs[par][e][...] - m_new).astype(BF16)
        acc_s[qi, e] = alpha * acc_s[qi, e] + _dot(vt, p)
        m_s[qi, e] = m_new

    half = t // 2

    def stage1_diag(qi, par, e):
        kb = key_block(qi * t)
        w = q_weights(qi, e)
        keys = lax.broadcasted_iota(jnp.int32, (half, half), 0)
        qpos = lax.broadcasted_iota(jnp.int32, (half, half), 1)
        tri = keys <= qpos
        s_left = jnp.where(tri, _dot(kb[0:half], w[:, 0:half]), -jnp.inf)
        s_right = _dot(kb, w[:, half:t])
        s_top = s_right[0:half]
        s_bot = jnp.where(tri, s_right[half:t], -jnp.inf)
        buf = bufs[par][e]
        buf[0:half, 0:half] = s_left
        buf[0:half, half:t] = s_top
        buf[half:t, half:t] = s_bot
        mc_s[par, e] = jnp.concatenate(
            [colmax(s_left), jnp.maximum(colmax(s_top), colmax(s_bot))], axis=1)

    def stage2_diag(qi, par, e):
        buf = bufs[par][e]
        vt = vt_s[e if mode == "fox" else 0, :, qi * t:(qi + 1) * t]
        m_prev = m_s[qi, e]
        m_new = jnp.maximum(m_prev, mc_s[par, e])
        alpha = jnp.exp2(m_prev - m_new)
        p_left = jnp.exp2(buf[0:half, 0:half] - m_new[:, 0:half]).astype(BF16)
        p_right = jnp.exp2(buf[:, half:t] - m_new[:, half:t]).astype(BF16)
        upd = jnp.concatenate([_dot(vt[:, 0:half], p_left), _dot(vt, p_right)], axis=1)
        acc_s[qi, e] = alpha * acc_s[qi, e] + upd
        m_s[qi, e] = m_new

    def item(w):
        return tab_ref[w, 0], tab_ref[w, 1]

    m_s[...] = jnp.full(m_s.shape, -jnp.inf, F32)
    acc_s[...] = jnp.zeros(acc_s.shape, F32)

    qa, ja = item(0)
    stage1(qa, ja, 0, 0)
    stage1(qa, ja, 0, 1)

    def body(i, c):
        w = 2 * i
        for par in range(2):
            qc, jc = item(w + par)
            qn, jn = item(w + par + 1)
            for e in range(2):
                stage1(qn, jn, 1 - par, e)
                stage2(qc, jc, par, e)
        return c

    lax.fori_loop(0, (nt * (nt - 1)) // 4, body, 0)

    stage1_diag(0, 0, 0)
    stage1_diag(0, 0, 1)
    for qi in range(nt):
        par = qi % 2
        for e in range(2):
            if qi + 1 < nt:
                stage1_diag(qi + 1, 1 - par, e)
            stage2_diag(qi, par, e)

    for qi in range(nt):
        o0 = acc_s[qi, 0, 0:dv] / acc_s[qi, 0, dv:dv + 1]
        o1 = acc_s[qi, 1, 0:dv] / acc_s[qi, 1, dv:dv + 1]
        if mode == "fox":
            o = jnp.concatenate([o0, o1], axis=0).T
        else:
            lv = lamv_ref[...]
            lam = (jnp.exp(jnp.sum(lv[0:1] * lv[1:2], axis=1, keepdims=True))
                   - jnp.exp(jnp.sum(lv[2:3] * lv[3:4], axis=1, keepdims=True)) + lambda_init)
            o = (o0 - lam * o1).T
            o = o * lax.rsqrt(jnp.mean(o * o, axis=-1, keepdims=True) + EPS)
            o = o * g_ref[...] * (1.0 - lambda_init)
        o_ref[0, qi * t:(qi + 1) * t, :] = o.astype(o_ref.dtype)


def _attention(mode, qkv, n_groups, extra, lambda_init=0.0):
    b, s, _ = qkv.shape
    t = ATTN_TILE
    nt = s // t
    assert (nt * (nt - 1)) % 4 == 0
    tab = jnp.asarray(_attn_items(nt), jnp.int32)
    in_specs = [
        pl.BlockSpec(memory_space=pltpu.SMEM),
        pl.BlockSpec((1, s, LANES), lambda i, h: (i, 0, h)),
        pl.BlockSpec((1, s, LANES), lambda i, h: (i, 0, n_groups + h)),
        pl.BlockSpec((1, s, LANES), lambda i, h: (i, 0, 2 * n_groups + h)),
    ]
    if mode == "fox":
        (kx,) = extra
        in_specs.append(pl.BlockSpec((1, s, LANES), lambda i, h: (i, 0, 0)))
        dv, nv = HEAD_DIM, 2
        pick_scratch = [pltpu.VMEM((2, LANES, t), BF16)]
    else:
        lamv, g = extra
        in_specs += [_const_spec(lamv.shape), _const_spec(g.shape)]
        dv, nv = LANES, 1
        pick_scratch = []
    return pl.pallas_call(
        functools.partial(_attn_kernel, mode=mode, lambda_init=lambda_init),
        grid=(b, n_groups),
        in_specs=in_specs,
        out_specs=pl.BlockSpec((1, s, LANES), lambda i, h: (i, 0, h)),
        out_shape=jax.ShapeDtypeStruct((b, s, n_groups * LANES), BF16),
        scratch_shapes=[
            pltpu.VMEM((nv, dv + ONES_ROWS, s), BF16),
            pltpu.VMEM((LANES, s), BF16),
            *pick_scratch,
            pltpu.VMEM((t, t), F32),
            pltpu.VMEM((t, t), F32),
            pltpu.VMEM((t, t), F32),
            pltpu.VMEM((t, t), F32),
            pltpu.VMEM((2, 2, 1, t), F32),
            pltpu.VMEM((nt, 2, 1, t), F32),
            pltpu.VMEM((nt, 2, dv + ONES_ROWS, t), F32),
        ],
        compiler_params=_params("parallel", "parallel"),
        name=mode + "_attn",
    )(tab, qkv, qkv, qkv, *extra)


FF_CHUNK = 1024


def _post_kernel(*refs, n_mix, final_norm):
    h_ref = refs[0]
    ys = refs[1:1 + n_mix]
    ws = refs[1 + n_mix:1 + 2 * n_mix]
    gx_ref, wq_ref, k_ref, v_ref, wo_ref, gm_ref, wu_ref, wd_ref, gf_ref, o_ref = refs[1 + 2 * n_mix:]
    d = h_ref.shape[2]
    dh = d // XATTN_HEADS

    h = h_ref[0]
    for y_ref, w_ref in zip(ys, ws):
        h = h + _dot(y_ref[0], w_ref[...])

    q = _dot(_rms(h, gx_ref[...]).astype(BF16), wq_ref[...]).astype(BF16)
    outs = []
    for e in range(XATTN_HEADS):
        sl = slice(e * dh, (e + 1) * dh)
        s = _dot_nt(q[:, sl], k_ref[0, :, sl]) * (dh ** -0.5)
        p = jnp.exp(s - jnp.max(s, axis=1, keepdims=True))
        p = p / jnp.sum(p, axis=1, keepdims=True)
        outs.append(_dot(p.astype(BF16), v_ref[0, :, sl]).astype(BF16))
    h = h + _dot(jnp.concatenate(outs, axis=1), wo_ref[...])

    xn = _rms(h, gm_ref[...]).astype(BF16)
    for c in range(wu_ref.shape[1] // FF_CHUNK):
        sl = slice(c * FF_CHUNK, (c + 1) * FF_CHUNK)
        u = jnp.maximum(_dot(xn, wu_ref[:, sl]), 0.0)
        h = h + _dot((u * u).astype(BF16), wd_ref[sl, :])
    if final_norm:
        h = _rms(h, gf_ref[...])
    o_ref[0] = h


def _post(h, ys, ws, gx, wq, kv, wo, gm, wu, wd, gf, final_norm):
    b, s, d = h.shape
    tm = ROW_TILE
    n_mix = len(ys)
    tile = lambda width: pl.BlockSpec((1, tm, width), lambda i, j: (i, j, 0))
    in_specs = [tile(d)] + [tile(y.shape[2]) for y in ys] + [_const_spec(w.shape) for w in ws]
    in_specs += [
        _const_spec((1, d)),
        _const_spec(wq.shape),
        pl.BlockSpec((1, N_MEM, d), lambda i, j: (i, 0, 0)),
        pl.BlockSpec((1, N_MEM, d), lambda i, j: (i, 0, 1)),
        _const_spec(wo.shape),
        _const_spec((1, d)),
        _const_spec(wu.shape),
        _const_spec(wd.shape),
        _const_spec((1, d)),
    ]
    return pl.pallas_call(
        functools.partial(_post_kernel, n_mix=n_mix, final_norm=final_norm),
        grid=(b, s // tm),
        in_specs=in_specs,
        out_specs=tile(d),
        out_shape=jax.ShapeDtypeStruct((b, s, d), F32),
        compiler_params=_params("parallel", "parallel"),
        name="post",
    )(h, *ys, *ws, gx, wq, kv, kv, wo, gm, wu, wd, gf)


def _block_diag(w):
    g, n, _ = w.shape
    eye = jnp.eye(g, dtype=w.dtype)
    return (eye[:, None, :, None] * w[:, :, None, :]).reshape(g * n, g * n)


def kernel(x, mem, g_mem, g_final, mix_norm_g, xattn_norm_g, mlp_norm_g, w_in_even, conv_w, conv_b, w_rgate, b_rgate, w_igate, b_igate, lru_lambda, fox_forget_b, w_out_even, w_in_odd, lambda_q1, lambda_k1, lambda_q2, lambda_k2, diff_norm_g, w_out_odd, xattn_wq, xattn_wkv, xattn_wo, w_up, w_down):
    b, s, d = x.shape
    depth = mix_norm_g.shape[0]
    m = b * s
    row = lambda v: v.reshape(1, -1).astype(F32)

    h = x
    for layer in range(depth):
        i = layer // 2
        if layer % 2 == 0:
            w_in = jnp.pad(w_in_even[i], ((0, 0), (0, 2688 - w_in_even.shape[2]))).astype(BF16)
            zl, qkv, kx = _even_proj(h, row(mix_norm_g[layer]), w_in,
                                     fox_forget_b[i].reshape(FOX_HEADS, 1).astype(F32))
            wg = jnp.concatenate([_block_diag(w_rgate[i]), _block_diag(w_igate[i])], axis=1).astype(BF16)
            bg = jnp.concatenate([b_rgate[i], b_igate[i]]).reshape(1, -1).astype(F32)
            y_lru = _lru(zl, conv_w[i].astype(F32), row(conv_b[i]), wg, bg, row(lru_lambda[i]))
            y_fox = _attention("fox", qkv, FOX_HEADS // 2, (kx,))
            w_out = w_out_even[i].astype(BF16)
            ys, ws = [y_lru, y_fox], [w_out[:LRU_WIDTH], w_out[LRU_WIDTH:]]
        else:
            z = _norm_proj(h.reshape(m, d), row(mix_norm_g[layer]), w_in_odd[i].astype(BF16), BF16,
                           q_cols=DIFF_HEADS * 2 * HEAD_DIM)
            lambda_init = 0.8 - 0.6 * math.exp(-0.3 * layer)
            lamv = jnp.stack([lambda_q1[i], lambda_k1[i], lambda_q2[i], lambda_k2[i]]).astype(F32)
            y = _attention("diff", z.reshape(b, s, -1), DIFF_HEADS, (lamv, row(diff_norm_g[i])),
                           lambda_init=lambda_init)
            ys, ws = [y], [w_out_odd[i].astype(BF16)]

        kv = _norm_proj(mem.reshape(b * N_MEM, d), row(g_mem), xattn_wkv[layer].astype(BF16), BF16)
        h = _post(h, ys, ws, row(xattn_norm_g[layer]), xattn_wq[layer].astype(BF16),
                  kv.reshape(b, N_MEM, 2 * d), xattn_wo[layer].astype(BF16),
                  row(mlp_norm_g[layer]), w_up[layer].astype(BF16), w_down[layer].astype(BF16),
                  row(g_final), final_norm=(layer == depth - 1))
    return h
```

```python
import functools
import math

import jax
import jax.numpy as jnp
from jax import lax
from jax.experimental import pallas as pl
from jax.experimental.pallas import tpu as pltpu

F32 = jnp.float32
BF16 = jnp.bfloat16

LANES = 128
SUBLANES = 8
VMEM_LIMIT = 56 * 1024 * 1024

EPS = 1e-6
N_MEM = 256
LRU_WIDTH = 512
LRU_BLOCKS = 8
CONV_WIDTH = 4
LRU_C = 8.0
HEAD_DIM = 64
FOX_HEADS = 8
DIFF_HEADS = 8
XATTN_HEADS = 4

LOG2E = 1.4426950408889634
QK_SCALE = HEAD_DIM ** -0.5 * LOG2E
ONES_ROWS = 16

ROW_TILE = 512
ATTN_TILE = 512
ITEMS_PER_TRIP = 14
LRU_TILE = 256


def _params(*sem):
    return pltpu.CompilerParams(dimension_semantics=sem, vmem_limit_bytes=VMEM_LIMIT)


def _rms(x, g):
    return x * lax.rsqrt(jnp.mean(x * x, axis=-1, keepdims=True) + EPS) * g


def _dot(a, b):
    return jnp.dot(a, b, preferred_element_type=F32)


def _dot_nt(a, b):
    return lax.dot_general(a, b, (((1,), (1,)), ((), ())), preferred_element_type=F32)


def _log_sigmoid(x):
    return jnp.minimum(x, 0.0) - jnp.log1p(jnp.exp(-jnp.abs(x)))


def _sigmoid(x):
    return 0.5 * jnp.tanh(0.5 * x) + 0.5


def _expm1_given_exp(x, u):
    near = jnp.where(u == 1.0, x, (u - 1.0) * x / jnp.log(u))
    return jnp.where(jnp.abs(x) < 0.5, near, u - 1.0)


def _const_spec(shape):
    nd = len(shape)
    return pl.BlockSpec(shape, lambda *_: (0,) * nd)


def _even_proj_kernel(x_ref, g_ref, w_ref, fb_ref, zl_ref, qkv_ref, kx_ref, carry_ref):
    si = pl.program_id(1)
    tm = x_ref.shape[1]
    xn = _rms(x_ref[0], g_ref[...]).astype(BF16)
    fl = _dot(xn, w_ref[:, 2560:2688])
    lf = _log_sigmoid(fl.T[0:FOX_HEADS, :] + fb_ref[...])

    @pl.when(si == 0)
    def _():
        carry_ref[...] = jnp.zeros_like(carry_ref)

    pos = lax.broadcasted_iota(jnp.int32, lf.shape, 1)
    shift = 1
    while shift < tm:
        lf = lf + jnp.where(pos >= shift, pltpu.roll(lf, shift, 1), 0.0)
        shift *= 2
    cum = lf + carry_ref[:, 0:1]
    carry_ref[...] = jnp.broadcast_to(cum[:, tm - 1:tm], carry_ref.shape)

    c = cum * (-LOG2E)
    hi = c.astype(BF16).astype(F32)
    mid = (c - hi).astype(BF16).astype(F32)
    lo = (c - hi - mid).astype(BF16).astype(F32)
    pieces = jnp.concatenate([hi, mid, lo, jnp.zeros((LANES - 3 * FOX_HEADS, tm), F32)], axis=0)
    kx_ref[0] = pieces.T.astype(BF16)

    zl_ref[0] = _dot(xn, w_ref[:, 0:1024])
    qkv_ref[0, :, 0:512] = (_dot(xn, w_ref[:, 1024:1536]) * QK_SCALE).astype(BF16)
    qkv_ref[0, :, 512:1536] = _dot(xn, w_ref[:, 1536:2560]).astype(BF16)


def _even_proj(h, g, w, fb):
    b, s, d = h.shape
    tm = ROW_TILE
    n = w.shape[1]
    return pl.pallas_call(
        _even_proj_kernel,
        grid=(b, s // tm),
        in_specs=[
            pl.BlockSpec((1, tm, d), lambda i, j: (i, j, 0)),
            _const_spec((1, d)),
            _const_spec((d, n)),
            _const_spec((FOX_HEADS, 1)),
        ],
        out_specs=[
            pl.BlockSpec((1, tm, 1024), lambda i, j: (i, j, 0)),
            pl.BlockSpec((1, tm, 1536), lambda i, j: (i, j, 0)),
            pl.BlockSpec((1, tm, LANES), lambda i, j: (i, j, 0)),
        ],
        out_shape=[
            jax.ShapeDtypeStruct((b, s, 1024), F32),
            jax.ShapeDtypeStruct((b, s, 1536), BF16),
            jax.ShapeDtypeStruct((b, s, LANES), BF16),
        ],
        scratch_shapes=[pltpu.VMEM((FOX_HEADS, LANES), F32)],
        compiler_params=_params("parallel", "arbitrary"),
        name="even_proj",
    )(h, g, w, fb)


def _norm_proj_kernel(x_ref, g_ref, w_ref, o_ref, *, q_cols):
    xn = _rms(x_ref[...], g_ref[...]).astype(BF16)
    if q_cols:
        o_ref[:, 0:q_cols] = (_dot(xn, w_ref[:, 0:q_cols]) * QK_SCALE).astype(o_ref.dtype)
    o_ref[:, q_cols:] = _dot(xn, w_ref[:, q_cols:]).astype(o_ref.dtype)


def _norm_proj(x2d, g, w, out_dtype, q_cols=0):
    m, d = x2d.shape
    n = w.shape[1]
    tm = min(ROW_TILE, m)
    return pl.pallas_call(
        functools.partial(_norm_proj_kernel, q_cols=q_cols),
        grid=(m // tm,),
        in_specs=[pl.BlockSpec((tm, d), lambda i: (i, 0)), _const_spec((1, d)), _const_spec((d, n))],
        out_specs=pl.BlockSpec((tm, n), lambda i: (i, 0)),
        out_shape=jax.ShapeDtypeStruct((m, n), out_dtype),
        compiler_params=_params("parallel"),
        name="norm_proj",
    )(x2d, g, w)


def _gelu_tanh(x):
    return 0.5 * x * (1.0 + jnp.tanh(math.sqrt(2.0 / math.pi) * (x + 0.044715 * (x * x * x))))


def _lru_kernel(xb_ref, gb_ref, cw_ref, cb_ref, wg_ref, bg_ref, lam_ref, y_ref,
                xbuf, a_s, u_s, hc):
    ti = pl.program_id(1)
    ts = xb_ref.shape[1]
    w = LRU_WIDTH

    @pl.when(ti == 0)
    def _():
        xbuf[0:SUBLANES, :] = jnp.zeros((SUBLANES, w), F32)
        hc[...] = jnp.zeros_like(hc)

    xbuf[SUBLANES:SUBLANES + ts, :] = xb_ref[0]
    xc = cb_ref[...]
    for k in range(CONV_WIDTH):
        lag = CONV_WIDTH - 1 - k
        xc = xc + xbuf[SUBLANES - lag:SUBLANES - lag + ts, :] * cw_ref[k:k + 1, :]
    xbuf[0:SUBLANES, :] = xbuf[ts:ts + SUBLANES, :]

    gates = _dot(xc.astype(BF16), wg_ref[...]) + bg_ref[...]
    r = _sigmoid(gates[:, 0:w])
    i = _sigmoid(gates[:, w:2 * w])
    log_a = LRU_C * r * _log_sigmoid(lam_ref[...])
    a = jnp.exp(log_a)
    a_s[...] = a
    v = -_expm1_given_exp(2.0 * log_a, a * a)
    sqrt_v = jnp.where(v > 0.0, v * lax.rsqrt(v), 0.0)
    u_s[...] = sqrt_v * (i * xc)

    row = lax.broadcasted_iota(jnp.int32, (SUBLANES, w), 0)

    def group(gidx, carry):
        off = pl.multiple_of(gidx * SUBLANES, SUBLANES)
        a = a_s[pl.ds(off, SUBLANES), :]
        u = u_s[pl.ds(off, SUBLANES), :]
        for sh in (1, 2, 4):
            keep = row >= sh
            u = u + a * jnp.where(keep, pltpu.roll(u, sh, 0), 0.0)
            a = a * jnp.where(keep, pltpu.roll(a, sh, 0), 1.0)
        hgrp = u + a * carry
        u_s[pl.ds(off, SUBLANES), :] = hgrp
        return jnp.broadcast_to(hgrp[SUBLANES - 1:SUBLANES, :], (SUBLANES, w))

    hc[...] = lax.fori_loop(0, ts // SUBLANES, group, hc[...], unroll=4)
    y_ref[0] = (u_s[...] * _gelu_tanh(gb_ref[0])).astype(y_ref.dtype)


def _lru(zl, cw, cb, wg, bg, lam):
    b, s, _ = zl.shape
    ts = LRU_TILE
    w = LRU_WIDTH
    return pl.pallas_call(
        _lru_kernel,
        grid=(b, s // ts),
        in_specs=[
            pl.BlockSpec((1, ts, w), lambda i, j: (i, j, 0)),
            pl.BlockSpec((1, ts, w), lambda i, j: (i, j, 1)),
            _const_spec((CONV_WIDTH, w)),
            _const_spec((1, w)),
            _const_spec((w, 2 * w)),
            _const_spec((1, 2 * w)),
            _const_spec((1, w)),
        ],
        out_specs=pl.BlockSpec((1, ts, w), lambda i, j: (i, j, 0)),
        out_shape=jax.ShapeDtypeStruct((b, s, w), BF16),
        scratch_shapes=[
            pltpu.VMEM((SUBLANES + ts, w), F32),
            pltpu.VMEM((ts, w), F32),
            pltpu.VMEM((ts, w), F32),
            pltpu.VMEM((SUBLANES, w), F32),
        ],
        compiler_params=_params("parallel", "arbitrary"),
        name="rglru",
    )(zl, zl, cw, cb, wg, bg, lam)


def _attn_items(nt):
    items = [(qi, j) for qi in range(1, nt) for j in range(qi)]
    return items + [items[-1]] * 2


def _attn_kernel(*refs, mode, lambda_init):
    if mode == "fox":
        tab_ref, q_ref, k_ref, v_ref, kx_ref, o_ref, vt_s, qt_s, pk_s, sa0, sa1, sb0, sb1, mc_s, m_s, acc_s = refs
    else:
        tab_ref, q_ref, k_ref, v_ref, lamv_ref, g_ref, o_ref, vt_s, qt_s, sa0, sa1, sb0, sb1, mc_s, m_s, acc_s = refs
    bufs = ((sa0, sa1), (sb0, sb1))
    hp = pl.program_id(1)
    t = ATTN_TILE
    s_len = k_ref.shape[1]
    nt = s_len // t
    dv = HEAD_DIM if mode == "fox" else LANES

    for c in range(nt):
        cols = slice(c * t, (c + 1) * t)
        vt = v_ref[0, cols, :].astype(F32).T.astype(BF16)
        ones = jnp.ones((ONES_ROWS, t), BF16)
        if mode == "fox":
            for e in range(2):
                vt_s[e, 0:dv, cols] = vt[e * dv:(e + 1) * dv]
                vt_s[e, dv:dv + ONES_ROWS, cols] = ones
        else:
            vt_s[0, 0:dv, cols] = vt
            vt_s[0, dv:dv + ONES_ROWS, cols] = ones
        qt_s[:, cols] = q_ref[0, cols, :].astype(F32).T.astype(BF16)
    if mode == "fox":
        rowi = lax.broadcasted_iota(jnp.int32, (LANES, t), 0)
        for e in range(2):
            head = 2 * hp + e
            pick = (rowi == head) | (rowi == head + FOX_HEADS) | (rowi == head + 2 * FOX_HEADS)
            pk_s[e] = jnp.where(pick, 1.0, 0.0).astype(BF16)

    def q_weights(qi, e):
        qt = qt_s[:, pl.ds(pl.multiple_of(qi * t, t), t)]
        zero = jnp.zeros((HEAD_DIM, t), BF16)
        parts = [qt[0:HEAD_DIM], zero] if e == 0 else [zero, qt[HEAD_DIM:LANES]]
        if mode == "fox":
            parts.append(pk_s[e])
        return jnp.concatenate(parts, axis=0)

    def colmax(x):
        acc = x[0:16]
        for i in range(1, x.shape[0] // 16):
            acc = jnp.maximum(acc, x[i * 16:(i + 1) * 16])
        return jnp.max(acc, axis=0, keepdims=True)

    def key_block(off):
        kb = k_ref[0, pl.ds(off, t), :]
        if mode == "fox":
            kb = jnp.concatenate([kb, kx_ref[0, pl.ds(off, t), :]], axis=1)
        return kb

    def stage1(qi, j, par, e):
        s = _dot(key_block(pl.multiple_of(j * t, t)), q_weights(qi, e))
        bufs[par][e][...] = s
        mc_s[par, e] = colmax(s)

    def stage2(qi, j, par, e):
        off = pl.multiple_of(j * t, t)
        vt = vt_s[e if mode == "fox" else 0, :, pl.ds(off, t)]
        m_prev = m_s[qi, e]
        m_new = jnp.maximum(m_prev, mc_s[par, e])
        alpha = jnp.exp2(m_prev - m_new)
        p = jnp.exp2(bufs[par][e][...] - m_new).astype(BF16)
        acc_s[qi, e] = alpha * acc_s[qi, e] + _dot(vt, p)
        m_s[qi, e] = m_new

    half = t // 2

    def stage1_diag(qi, par, e):
        kb = key_block(qi * t)
        w = q_weights(qi, e)
        keys = lax.broadcasted_iota(jnp.int32, (half, half), 0)
        qpos = lax.broadcasted_iota(jnp.int32, (half, half), 1)
        tri = keys <= qpos
        s_left = jnp.where(tri, _dot(kb[0:half], w[:, 0:half]), -jnp.inf)
        s_right = _dot(kb, w[:, half:t])
        s_top = s_right[0:half]
        s_bot = jnp.where(tri, s_right[half:t], -jnp.inf)
        buf = bufs[par][e]
        buf[0:half, 0:half] = s_left
        buf[0:half, half:t] = s_top
        buf[half:t, half:t] = s_bot
        mc_s[par, e] = jnp.concatenate(
            [colmax(s_left), jnp.maximum(colmax(s_top), colmax(s_bot))], axis=1)

    def stage2_diag(qi, par, e):
        buf = bufs[par][e]
        vt = vt_s[e if mode == "fox" else 0, :, qi * t:(qi + 1) * t]
        m_prev = m_s[qi, e]
        m_new = jnp.maximum(m_prev, mc_s[par, e])
        alpha = jnp.exp2(m_prev - m_new)
        p_left = jnp.exp2(buf[0:half, 0:half] - m_new[:, 0:half]).astype(BF16)
        p_right = jnp.exp2(buf[:, half:t] - m_new[:, half:t]).astype(BF16)
        upd = jnp.concatenate([_dot(vt[:, 0:half], p_left), _dot(vt, p_right)], axis=1)
        acc_s[qi, e] = alpha * acc_s[qi, e] + upd
        m_s[qi, e] = m_new

    def item(w):
        return tab_ref[w, 0], tab_ref[w, 1]

    m_s[...] = jnp.full(m_s.shape, -jnp.inf, F32)
    acc_s[...] = jnp.zeros(acc_s.shape, F32)

    qa, ja = item(0)
    stage1(qa, ja, 0, 0)
    stage1(qa, ja, 0, 1)

    def body(i, c):
        w = ITEMS_PER_TRIP * i
        for k in range(ITEMS_PER_TRIP):
            qc, jc = item(w + k)
            qn, jn = item(w + k + 1)
            for e in range(2):
                stage1(qn, jn, (k + 1) % 2, e)
                stage2(qc, jc, k % 2, e)
        return c

    lax.fori_loop(0, (nt * (nt - 1)) // (2 * ITEMS_PER_TRIP), body, 0)

    stage1_diag(0, 0, 0)
    stage1_diag(0, 0, 1)
    for qi in range(nt):
        par = qi % 2
        for e in range(2):
            if qi + 1 < nt:
                stage1_diag(qi + 1, 1 - par, e)
            stage2_diag(qi, par, e)

    for qi in range(nt):
        o0 = acc_s[qi, 0, 0:dv] / acc_s[qi, 0, dv:dv + 1]
        o1 = acc_s[qi, 1, 0:dv] / acc_s[qi, 1, dv:dv + 1]
        if mode == "fox":
            o = jnp.concatenate([o0, o1], axis=0).T
        else:
            lv = lamv_ref[...]
            lam = (jnp.exp(jnp.sum(lv[0:1] * lv[1:2], axis=1, keepdims=True))
                   - jnp.exp(jnp.sum(lv[2:3] * lv[3:4], axis=1, keepdims=True)) + lambda_init)
            o = (o0 - lam * o1).T
            o = o * lax.rsqrt(jnp.mean(o * o, axis=-1, keepdims=True) + EPS)
            o = o * g_ref[...] * (1.0 - lambda_init)
        o_ref[0, qi * t:(qi + 1) * t, :] = o.astype(o_ref.dtype)


def _attention(mode, qkv, n_groups, extra, lambda_init=0.0):
    b, s, _ = qkv.shape
    t = ATTN_TILE
    nt = s // t
    assert (nt * (nt - 1)) % 4 == 0
    tab = jnp.asarray(_attn_items(nt), jnp.int32)
    in_specs = [
        pl.BlockSpec(memory_space=pltpu.SMEM),
        pl.BlockSpec((1, s, LANES), lambda i, h: (i, 0, h)),
        pl.BlockSpec((1, s, LANES), lambda i, h: (i, 0, n_groups + h)),
        pl.BlockSpec((1, s, LANES), lambda i, h: (i, 0, 2 * n_groups + h)),
    ]
    if mode == "fox":
        (kx,) = extra
        in_specs.append(pl.BlockSpec((1, s, LANES), lambda i, h: (i, 0, 0)))
        dv, nv = HEAD_DIM, 2
        pick_scratch = [pltpu.VMEM((2, LANES, t), BF16)]
    else:
        lamv, g = extra
        in_specs += [_const_spec(lamv.shape), _const_spec(g.shape)]
        dv, nv = LANES, 1
        pick_scratch = []
    return pl.pallas_call(
        functools.partial(_attn_kernel, mode=mode, lambda_init=lambda_init),
        grid=(b, n_groups),
        in_specs=in_specs,
        out_specs=pl.BlockSpec((1, s, LANES), lambda i, h: (i, 0, h)),
        out_shape=jax.ShapeDtypeStruct((b, s, n_groups * LANES), BF16),
        scratch_shapes=[
            pltpu.VMEM((nv, dv + ONES_ROWS, s), BF16),
            pltpu.VMEM((LANES, s), BF16),
            *pick_scratch,
            pltpu.VMEM((t, t), F32),
            pltpu.VMEM((t, t), F32),
            pltpu.VMEM((t, t), F32),
            pltpu.VMEM((t, t), F32),
            pltpu.VMEM((2, 2, 1, t), F32),
            pltpu.VMEM((nt, 2, 1, t), F32),
            pltpu.VMEM((nt, 2, dv + ONES_ROWS, t), F32),
        ],
        compiler_params=_params("parallel", "parallel"),
        name=mode + "_attn",
    )(tab, qkv, qkv, qkv, *extra)


FF_CHUNK = 1024


def _post_kernel(*refs, n_mix, final_norm):
    h_ref = refs[0]
    ys = refs[1:1 + n_mix]
    ws = refs[1 + n_mix:1 + 2 * n_mix]
    gx_ref, wq_ref, k_ref, v_ref, wo_ref, gm_ref, wu_ref, wd_ref, gf_ref, o_ref = refs[1 + 2 * n_mix:]
    d = h_ref.shape[2]
    dh = d // XATTN_HEADS

    h = h_ref[0]
    for y_ref, w_ref in zip(ys, ws):
        h = h + _dot(y_ref[0], w_ref[...])

    q = _dot(_rms(h, gx_ref[...]).astype(BF16), wq_ref[...]).astype(BF16)
    outs = []
    for e in range(XATTN_HEADS):
        sl = slice(e * dh, (e + 1) * dh)
        s = _dot_nt(q[:, sl], k_ref[0, :, sl]) * (dh ** -0.5)
        p = jnp.exp(s - jnp.max(s, axis=1, keepdims=True))
        p = p / jnp.sum(p, axis=1, keepdims=True)
        outs.append(_dot(p.astype(BF16), v_ref[0, :, sl]).astype(BF16))
    h = h + _dot(jnp.concatenate(outs, axis=1), wo_ref[...])

    xn = _rms(h, gm_ref[...]).astype(BF16)
    for c in range(wu_ref.shape[1] // FF_CHUNK):
        sl = slice(c * FF_CHUNK, (c + 1) * FF_CHUNK)
        u = jnp.maximum(_dot(xn, wu_ref[:, sl]), 0.0)
        h = h + _dot((u * u).astype(BF16), wd_ref[sl, :])
    if final_norm:
        h = _rms(h, gf_ref[...])
    o_ref[0] = h


def _post(h, ys, ws, gx, wq, kv, wo, gm, wu, wd, gf, final_norm):
    b, s, d = h.shape
    tm = ROW_TILE
    n_mix = len(ys)
    tile = lambda width: pl.BlockSpec((1, tm, width), lambda i, j: (i, j, 0))
    in_specs = [tile(d)] + [tile(y.shape[2]) for y in ys] + [_const_spec(w.shape) for w in ws]
    in_specs += [
        _const_spec((1, d)),
        _const_spec(wq.shape),
        pl.BlockSpec((1, N_MEM, d), lambda i, j: (i, 0, 0)),
        pl.BlockSpec((1, N_MEM, d), lambda i, j: (i, 0, 1)),
        _const_spec(wo.shape),
        _const_spec((1, d)),
        _const_spec(wu.shape),
        _const_spec(wd.shape),
        _const_spec((1, d)),
    ]
    return pl.pallas_call(
        functools.partial(_post_kernel, n_mix=n_mix, final_norm=final_norm),
        grid=(b, s // tm),
        in_specs=in_specs,
        out_specs=tile(d),
        out_shape=jax.ShapeDtypeStruct((b, s, d), F32),
        compiler_params=_params("parallel", "parallel"),
        name="post",
    )(h, *ys, *ws, gx, wq, kv, kv, wo, gm, wu, wd, gf)


def _block_diag(w):
    g, n, _ = w.shape
    eye = jnp.eye(g, dtype=w.dtype)
    return (eye[:, None, :, None] * w[:, :, None, :]).reshape(g * n, g * n)


def kernel(x, mem, g_mem, g_final, mix_norm_g, xattn_norm_g, mlp_norm_g, w_in_even, conv_w, conv_b, w_rgate, b_rgate, w_igate, b_igate, lru_lambda, fox_forget_b, w_out_even, w_in_odd, lambda_q1, lambda_k1, lambda_q2, lambda_k2, diff_norm_g, w_out_odd, xattn_wq, xattn_wkv, xattn_wo, w_up, w_down):
    b, s, d = x.shape
    depth = mix_norm_g.shape[0]
    m = b * s
    row = lambda v: v.reshape(1, -1).astype(F32)

    h = x
    for layer in range(depth):
        i = layer // 2
        if layer % 2 == 0:
            w_in = jnp.pad(w_in_even[i], ((0, 0), (0, 2688 - w_in_even.shape[2]))).astype(BF16)
            zl, qkv, kx = _even_proj(h, row(mix_norm_g[layer]), w_in,
                                     fox_forget_b[i].reshape(FOX_HEADS, 1).astype(F32))
            wg = jnp.concatenate([_block_diag(w_rgate[i]), _block_diag(w_igate[i])], axis=1).astype(BF16)
            bg = jnp.concatenate([b_rgate[i], b_igate[i]]).reshape(1, -1).astype(F32)
            y_lru = _lru(zl, conv_w[i].astype(F32), row(conv_b[i]), wg, bg, row(lru_lambda[i]))
            y_fox = _attention("fox", qkv, FOX_HEADS // 2, (kx,))
            w_out = w_out_even[i].astype(BF16)
            ys, ws = [y_lru, y_fox], [w_out[:LRU_WIDTH], w_out[LRU_WIDTH:]]
        else:
            z = _norm_proj(h.reshape(m, d), row(mix_norm_g[layer]), w_in_odd[i].astype(BF16), BF16,
                           q_cols=DIFF_HEADS * 2 * HEAD_DIM)
            lambda_init = 0.8 - 0.6 * math.exp(-0.3 * layer)
            lamv = jnp.stack([lambda_q1[i], lambda_k1[i], lambda_q2[i], lambda_k2[i]]).astype(F32)
            y = _attention("diff", z.reshape(b, s, -1), DIFF_HEADS, (lamv, row(diff_norm_g[i])),
                           lambda_init=lambda_init)
            ys, ws = [y], [w_out_odd[i].astype(BF16)]

        kv = _norm_proj(mem.reshape(b * N_MEM, d), row(g_mem), xattn_wkv[layer].astype(BF16), BF16)
        h = _post(h, ys, ws, row(xattn_norm_g[layer]), xattn_wq[layer].astype(BF16),
                  kv.reshape(b, N_MEM, 2 * d), xattn_wo[layer].astype(BF16),
                  row(mlp_norm_g[layer]), w_up[layer].astype(BF16), w_down[layer].astype(BF16),
                  row(g_final), final_norm=(layer == depth - 1))
    return h
```

```python
import functools
import math

import jax
import jax.numpy as jnp
from jax import lax
from jax.experimental import pallas as pl
from jax.experimental.pallas import tpu as pltpu

F32 = jnp.float32
BF16 = jnp.bfloat16

LANES = 128
SUBLANES = 8
VMEM_LIMIT = 56 * 1024 * 1024

EPS = 1e-6
N_MEM = 256
LRU_WIDTH = 512
LRU_BLOCKS = 8
CONV_WIDTH = 4
LRU_C = 8.0
HEAD_DIM = 64
FOX_HEADS = 8
DIFF_HEADS = 8
XATTN_HEADS = 4

LOG2E = 1.4426950408889634
QK_SCALE = HEAD_DIM ** -0.5 * LOG2E
ONES_ROWS = 16

ROW_TILE = 512
PROJ_TILE = 1024
ATTN_TILE = 512
ITEMS_PER_TRIP = 14
LRU_TILE = 256


def _params(*sem):
    return pltpu.CompilerParams(dimension_semantics=sem, vmem_limit_bytes=VMEM_LIMIT)


def _rms(x, g):
    return x * lax.rsqrt(jnp.mean(x * x, axis=-1, keepdims=True) + EPS) * g


def _dot(a, b):
    return jnp.dot(a, b, preferred_element_type=F32)


def _dot_nt(a, b):
    return lax.dot_general(a, b, (((1,), (1,)), ((), ())), preferred_element_type=F32)


def _log_sigmoid(x):
    return jnp.minimum(x, 0.0) - jnp.log1p(jnp.exp(-jnp.abs(x)))


def _sigmoid(x):
    return 0.5 * jnp.tanh(0.5 * x) + 0.5


def _expm1_given_exp(x, u):
    near = jnp.where(u == 1.0, x, (u - 1.0) * x / jnp.log(u))
    return jnp.where(jnp.abs(x) < 0.5, near, u - 1.0)


def _const_spec(shape):
    nd = len(shape)
    return pl.BlockSpec(shape, lambda *_: (0,) * nd)


def _even_proj_kernel(x_ref, g_ref, w_ref, fb_ref, zl_ref, qkv_ref, kx_ref, carry_ref):
    si = pl.program_id(1)
    tm = x_ref.shape[1]
    xn = _rms(x_ref[0], g_ref[...]).astype(BF16)
    fl = _dot(xn, w_ref[:, 2560:2688])
    lf = _log_sigmoid(fl.T[0:FOX_HEADS, :] + fb_ref[...])

    @pl.when(si == 0)
    def _():
        carry_ref[...] = jnp.zeros_like(carry_ref)

    pos = lax.broadcasted_iota(jnp.int32, lf.shape, 1)
    shift = 1
    while shift < tm:
        lf = lf + jnp.where(pos >= shift, pltpu.roll(lf, shift, 1), 0.0)
        shift *= 2
    cum = lf + carry_ref[:, 0:1]
    carry_ref[...] = jnp.broadcast_to(cum[:, tm - 1:tm], carry_ref.shape)

    c = cum * (-LOG2E)
    hi = c.astype(BF16).astype(F32)
    mid = (c - hi).astype(BF16).astype(F32)
    lo = (c - hi - mid).astype(BF16).astype(F32)
    pieces = jnp.concatenate([hi, mid, lo, jnp.zeros((LANES - 3 * FOX_HEADS, tm), F32)], axis=0)
    kx_ref[0] = pieces.T.astype(BF16)

    zl_ref[0] = _dot(xn, w_ref[:, 0:1024])
    qkv_ref[0, :, 0:512] = (_dot(xn, w_ref[:, 1024:1536]) * QK_SCALE).astype(BF16)
    qkv_ref[0, :, 512:1536] = _dot(xn, w_ref[:, 1536:2560]).astype(BF16)


def _even_proj(h, g, w, fb):
    b, s, d = h.shape
    tm = PROJ_TILE
    n = w.shape[1]
    return pl.pallas_call(
        _even_proj_kernel,
        grid=(b, s // tm),
        in_specs=[
            pl.BlockSpec((1, tm, d), lambda i, j: (i, j, 0)),
            _const_spec((1, d)),
            _const_spec((d, n)),
            _const_spec((FOX_HEADS, 1)),
        ],
        out_specs=[
            pl.BlockSpec((1, tm, 1024), lambda i, j: (i, j, 0)),
            pl.BlockSpec((1, tm, 1536), lambda i, j: (i, j, 0)),
            pl.BlockSpec((1, tm, LANES), lambda i, j: (i, j, 0)),
        ],
        out_shape=[
            jax.ShapeDtypeStruct((b, s, 1024), F32),
            jax.ShapeDtypeStruct((b, s, 1536), BF16),
            jax.ShapeDtypeStruct((b, s, LANES), BF16),
        ],
        scratch_shapes=[pltpu.VMEM((FOX_HEADS, LANES), F32)],
        compiler_params=_params("parallel", "arbitrary"),
        name="even_proj",
    )(h, g, w, fb)


def _norm_proj_kernel(x_ref, g_ref, w_ref, o_ref, *, q_cols):
    xn = _rms(x_ref[...], g_ref[...]).astype(BF16)
    if q_cols:
        o_ref[:, 0:q_cols] = (_dot(xn, w_ref[:, 0:q_cols]) * QK_SCALE).astype(o_ref.dtype)
    o_ref[:, q_cols:] = _dot(xn, w_ref[:, q_cols:]).astype(o_ref.dtype)


def _norm_proj(x2d, g, w, out_dtype, q_cols=0):
    m, d = x2d.shape
    n = w.shape[1]
    tm = PROJ_TILE
    return pl.pallas_call(
        functools.partial(_norm_proj_kernel, q_cols=q_cols),
        grid=(m // tm,),
        in_specs=[pl.BlockSpec((tm, d), lambda i: (i, 0)), _const_spec((1, d)), _const_spec((d, n))],
        out_specs=pl.BlockSpec((tm, n), lambda i: (i, 0)),
        out_shape=jax.ShapeDtypeStruct((m, n), out_dtype),
        compiler_params=_params("parallel"),
        name="norm_proj",
    )(x2d, g, w)


def _gelu_tanh(x):
    return 0.5 * x * (1.0 + jnp.tanh(math.sqrt(2.0 / math.pi) * (x + 0.044715 * (x * x * x))))


def _lru_kernel(xb_ref, gb_ref, cw_ref, cb_ref, wg_ref, bg_ref, lam_ref, y_ref,
                xbuf, a_s, u_s, hc):
    ti = pl.program_id(1)
    ts = xb_ref.shape[1]
    w = LRU_WIDTH

    @pl.when(ti == 0)
    def _():
        xbuf[0:SUBLANES, :] = jnp.zeros((SUBLANES, w), F32)
        hc[...] = jnp.zeros_like(hc)

    xbuf[SUBLANES:SUBLANES + ts, :] = xb_ref[0]
    xc = cb_ref[...]
    for k in range(CONV_WIDTH):
        lag = CONV_WIDTH - 1 - k
        xc = xc + xbuf[SUBLANES - lag:SUBLANES - lag + ts, :] * cw_ref[k:k + 1, :]
    xbuf[0:SUBLANES, :] = xbuf[ts:ts + SUBLANES, :]

    gates = _dot(xc.astype(BF16), wg_ref[...]) + bg_ref[...]
    r = _sigmoid(gates[:, 0:w])
    i = _sigmoid(gates[:, w:2 * w])
    log_a = LRU_C * r * _log_sigmoid(lam_ref[...])
    a = jnp.exp(log_a)
    a_s[...] = a
    v = -_expm1_given_exp(2.0 * log_a, a * a)
    sqrt_v = jnp.where(v > 0.0, v * lax.rsqrt(v), 0.0)
    u_s[...] = sqrt_v * (i * xc)

    row = lax.broadcasted_iota(jnp.int32, (SUBLANES, w), 0)

    def group(gidx, carry):
        off = pl.multiple_of(gidx * SUBLANES, SUBLANES)
        a = a_s[pl.ds(off, SUBLANES), :]
        u = u_s[pl.ds(off, SUBLANES), :]
        for sh in (1, 2, 4):
            keep = row >= sh
            u = u + a * jnp.where(keep, pltpu.roll(u, sh, 0), 0.0)
            a = a * jnp.where(keep, pltpu.roll(a, sh, 0), 1.0)
        hgrp = u + a * carry
        u_s[pl.ds(off, SUBLANES), :] = hgrp
        return jnp.broadcast_to(hgrp[SUBLANES - 1:SUBLANES, :], (SUBLANES, w))

    hc[...] = lax.fori_loop(0, ts // SUBLANES, group, hc[...], unroll=4)
    y_ref[0] = (u_s[...] * _gelu_tanh(gb_ref[0])).astype(y_ref.dtype)


def _lru(zl, cw, cb, wg, bg, lam):
    b, s, _ = zl.shape
    ts = LRU_TILE
    w = LRU_WIDTH
    return pl.pallas_call(
        _lru_kernel,
        grid=(b, s // ts),
        in_specs=[
            pl.BlockSpec((1, ts, w), lambda i, j: (i, j, 0)),
            pl.BlockSpec((1, ts, w), lambda i, j: (i, j, 1)),
            _const_spec((CONV_WIDTH, w)),
            _const_spec((1, w)),
            _const_spec((w, 2 * w)),
            _const_spec((1, 2 * w)),
            _const_spec((1, w)),
        ],
        out_specs=pl.BlockSpec((1, ts, w), lambda i, j: (i, j, 0)),
        out_shape=jax.ShapeDtypeStruct((b, s, w), BF16),
        scratch_shapes=[
            pltpu.VMEM((SUBLANES + ts, w), F32),
            pltpu.VMEM((ts, w), F32),
            pltpu.VMEM((ts, w), F32),
            pltpu.VMEM((SUBLANES, w), F32),
        ],
        compiler_params=_params("parallel", "arbitrary"),
        name="rglru",
    )(zl, zl, cw, cb, wg, bg, lam)


def _attn_items(nt):
    items = [(qi, j) for qi in range(1, nt) for j in range(qi)]
    return items + [items[-1]] * 2


def _attn_kernel(*refs, mode, lambda_init):
    if mode == "fox":
        tab_ref, q_ref, k_ref, v_ref, kx_ref, o_ref, vt_s, qt_s, pk_s, sa0, sa1, sb0, sb1, mc_s, m_s, acc_s = refs
    else:
        tab_ref, q_ref, k_ref, v_ref, lamv_ref, g_ref, o_ref, vt_s, qt_s, sa0, sa1, sb0, sb1, mc_s, m_s, acc_s = refs
    bufs = ((sa0, sa1), (sb0, sb1))
    hp = pl.program_id(1)
    t = ATTN_TILE
    s_len = k_ref.shape[1]
    nt = s_len // t
    dv = HEAD_DIM if mode == "fox" else LANES

    for c in range(nt):
        cols = slice(c * t, (c + 1) * t)
        vt = v_ref[0, cols, :].astype(F32).T.astype(BF16)
        ones = jnp.ones((ONES_ROWS, t), BF16)
        if mode == "fox":
            for e in range(2):
                vt_s[e, 0:dv, cols] = vt[e * dv:(e + 1) * dv]
                vt_s[e, dv:dv + ONES_ROWS, cols] = ones
        else:
            vt_s[0, 0:dv, cols] = vt
            vt_s[0, dv:dv + ONES_ROWS, cols] = ones
        qt_s[:, cols] = q_ref[0, cols, :].astype(F32).T.astype(BF16)
    if mode == "fox":
        rowi = lax.broadcasted_iota(jnp.int32, (LANES, t), 0)
        for e in range(2):
            head = 2 * hp + e
            pick = (rowi == head) | (rowi == head + FOX_HEADS) | (rowi == head + 2 * FOX_HEADS)
            pk_s[e] = jnp.where(pick, 1.0, 0.0).astype(BF16)

    def q_weights(qi, e):
        qt = qt_s[:, pl.ds(pl.multiple_of(qi * t, t), t)]
        zero = jnp.zeros((HEAD_DIM, t), BF16)
        parts = [qt[0:HEAD_DIM], zero] if e == 0 else [zero, qt[HEAD_DIM:LANES]]
        if mode == "fox":
            parts.append(pk_s[e])
        return jnp.concatenate(parts, axis=0)

    def colmax(x):
        acc = x[0:16]
        for i in range(1, x.shape[0] // 16):
            acc = jnp.maximum(acc, x[i * 16:(i + 1) * 16])
        return jnp.max(acc, axis=0, keepdims=True)

    def key_block(off):
        kb = k_ref[0, pl.ds(off, t), :]
        if mode == "fox":
            kb = jnp.concatenate([kb, kx_ref[0, pl.ds(off, t), :]], axis=1)
        return kb

    def stage1(qi, j, par, e):
        s = _dot(key_block(pl.multiple_of(j * t, t)), q_weights(qi, e))
        bufs[par][e][...] = s
        mc_s[par, e] = colmax(s)

    def stage2(qi, j, par, e):
        off = pl.multiple_of(j * t, t)
        vt = vt_s[e if mode == "fox" else 0, :, pl.ds(off, t)]
        m_prev = m_s[qi, e]
        m_new = jnp.maximum(m_prev, mc_s[par, e])
        alpha = jnp.exp2(m_prev - m_new)
        p = jnp.exp2(bufs[par][e][...] - m_new).astype(BF16)
        acc_s[qi, e] = alpha * acc_s[qi, e] + _dot(vt, p)
        m_s[qi, e] = m_new

    half = t // 2

    def stage1_diag(qi, par, e):
        kb = key_block(qi * t)
        w = q_weights(qi, e)
        keys = lax.broadcasted_iota(jnp.int32, (half, half), 0)
        qpos = lax.broadcasted_iota(jnp.int32, (half, half), 1)
        tri = keys <= qpos
        s_left = jnp.where(tri, _dot(kb[0:half], w[:, 0:half]), -jnp.inf)
        s_right = _dot(kb, w[:, half:t])
        s_top = s_right[0:half]
        s_bot = jnp.where(tri, s_right[half:t], -jnp.inf)
        buf = bufs[par][e]
        buf[0:half, 0:half] = s_left
        buf[0:half, half:t] = s_top
        buf[half:t, half:t] = s_bot
        mc_s[par, e] = jnp.concatenate(
            [colmax(s_left), jnp.maximum(colmax(s_top), colmax(s_bot))], axis=1)

    def stage2_diag(qi, par, e):
        buf = bufs[par][e]
        vt = vt_s[e if mode == "fox" else 0, :, qi * t:(qi + 1) * t]
        m_prev = m_s[qi, e]
        m_new = jnp.maximum(m_prev, mc_s[par, e])
        alpha = jnp.exp2(m_prev - m_new)
        p_left = jnp.exp2(buf[0:half, 0:half] - m_new[:, 0:half]).astype(BF16)
        p_right = jnp.exp2(buf[:, half:t] - m_new[:, half:t]).astype(BF16)
        upd = jnp.concatenate([_dot(vt[:, 0:half], p_left), _dot(vt, p_right)], axis=1)
        acc_s[qi, e] = alpha * acc_s[qi, e] + upd
        m_s[qi, e] = m_new

    def item(w):
        return tab_ref[w, 0], tab_ref[w, 1]

    m_s[...] = jnp.full(m_s.shape, -jnp.inf, F32)
    acc_s[...] = jnp.zeros(acc_s.shape, F32)

    qa, ja = item(0)
    stage1(qa, ja, 0, 0)
    stage1(qa, ja, 0, 1)

    def body(i, c):
        w = ITEMS_PER_TRIP * i
        for k in range(ITEMS_PER_TRIP):
            qc, jc = item(w + k)
            qn, jn = item(w + k + 1)
            for e in range(2):
                stage1(qn, jn, (k + 1) % 2, e)
                stage2(qc, jc, k % 2, e)
        return c

    lax.fori_loop(0, (nt * (nt - 1)) // (2 * ITEMS_PER_TRIP), body, 0)

    stage1_diag(0, 0, 0)
    stage1_diag(0, 0, 1)
    for qi in range(nt):
        par = qi % 2
        for e in range(2):
            if qi + 1 < nt:
                stage1_diag(qi + 1, 1 - par, e)
            stage2_diag(qi, par, e)

    for qi in range(nt):
        o0 = acc_s[qi, 0, 0:dv] / acc_s[qi, 0, dv:dv + 1]
        o1 = acc_s[qi, 1, 0:dv] / acc_s[qi, 1, dv:dv + 1]
        if mode == "fox":
            o = jnp.concatenate([o0, o1], axis=0).T
        else:
            lv = lamv_ref[...]
            lam = (jnp.exp(jnp.sum(lv[0:1] * lv[1:2], axis=1, keepdims=True))
                   - jnp.exp(jnp.sum(lv[2:3] * lv[3:4], axis=1, keepdims=True)) + lambda_init)
            o = (o0 - lam * o1).T
            o = o * lax.rsqrt(jnp.mean(o * o, axis=-1, keepdims=True) + EPS)
            o = o * g_ref[...] * (1.0 - lambda_init)
        o_ref[0, qi * t:(qi + 1) * t, :] = o.astype(o_ref.dtype)


def _attention(mode, qkv, n_groups, extra, lambda_init=0.0):
    b, s, _ = qkv.shape
    t = ATTN_TILE
    nt = s // t
    assert (nt * (nt - 1)) % 4 == 0
    tab = jnp.asarray(_attn_items(nt), jnp.int32)
    in_specs = [
        pl.BlockSpec(memory_space=pltpu.SMEM),
        pl.BlockSpec((1, s, LANES), lambda i, h: (i, 0, h)),
        pl.BlockSpec((1, s, LANES), lambda i, h: (i, 0, n_groups + h)),
        pl.BlockSpec((1, s, LANES), lambda i, h: (i, 0, 2 * n_groups + h)),
    ]
    if mode == "fox":
        (kx,) = extra
        in_specs.append(pl.BlockSpec((1, s, LANES), lambda i, h: (i, 0, 0)))
        dv, nv = HEAD_DIM, 2
        pick_scratch = [pltpu.VMEM((2, LANES, t), BF16)]
    else:
        lamv, g = extra
        in_specs += [_const_spec(lamv.shape), _const_spec(g.shape)]
        dv, nv = LANES, 1
        pick_scratch = []
    return pl.pallas_call(
        functools.partial(_attn_kernel, mode=mode, lambda_init=lambda_init),
        grid=(b, n_groups),
        in_specs=in_specs,
        out_specs=pl.BlockSpec((1, s, LANES), lambda i, h: (i, 0, h)),
        out_shape=jax.ShapeDtypeStruct((b, s, n_groups * LANES), BF16),
        scratch_shapes=[
            pltpu.VMEM((nv, dv + ONES_ROWS, s), BF16),
            pltpu.VMEM((LANES, s), BF16),
            *pick_scratch,
            pltpu.VMEM((t, t), F32),
            pltpu.VMEM((t, t), F32),
            pltpu.VMEM((t, t), F32),
            pltpu.VMEM((t, t), F32),
            pltpu.VMEM((2, 2, 1, t), F32),
            pltpu.VMEM((nt, 2, 1, t), F32),
            pltpu.VMEM((nt, 2, dv + ONES_ROWS, t), F32),
        ],
        compiler_params=_params("parallel", "parallel"),
        name=mode + "_attn",
    )(tab, qkv, qkv, qkv, *extra)


FF_CHUNK = 1024


def _post_kernel(*refs, n_mix, final_norm):
    h_ref = refs[0]
    ys = refs[1:1 + n_mix]
    ws = refs[1 + n_mix:1 + 2 * n_mix]
    gx_ref, wq_ref, k_ref, v_ref, wo_ref, gm_ref, wu_ref, wd_ref, gf_ref, o_ref = refs[1 + 2 * n_mix:]
    d = h_ref.shape[2]
    dh = d // XATTN_HEADS

    h = h_ref[0]
    for y_ref, w_ref in zip(ys, ws):
        h = h + _dot(y_ref[0], w_ref[...])

    q = _dot(_rms(h, gx_ref[...]).astype(BF16), wq_ref[...]).astype(BF16)
    outs = []
    for e in range(XATTN_HEADS):
        sl = slice(e * dh, (e + 1) * dh)
        s = _dot_nt(q[:, sl], k_ref[0, :, sl]) * (dh ** -0.5)
        p = jnp.exp(s - jnp.max(s, axis=1, keepdims=True))
        p = p / jnp.sum(p, axis=1, keepdims=True)
        outs.append(_dot(p.astype(BF16), v_ref[0, :, sl]).astype(BF16))
    h = h + _dot(jnp.concatenate(outs, axis=1), wo_ref[...])

    xn = _rms(h, gm_ref[...]).astype(BF16)
    for c in range(wu_ref.shape[1] // FF_CHUNK):
        sl = slice(c * FF_CHUNK, (c + 1) * FF_CHUNK)
        u = jnp.maximum(_dot(xn, wu_ref[:, sl]), 0.0)
        h = h + _dot((u * u).astype(BF16), wd_ref[sl, :])
    if final_norm:
        h = _rms(h, gf_ref[...])
    o_ref[0] = h


def _layer_spec(shape, layer, block=0):
    return pl.BlockSpec((None,) + tuple(shape), lambda *_: (layer, block, 0))


def _post(h, ys, w_mix, mix_layer, gx, wq, kv, wo, gm, wu, wd, layer, gf, final_norm):
    b, s, d = h.shape
    tm = ROW_TILE
    n_mix = len(ys)
    widths = [y.shape[2] for y in ys]
    assert all(wd_ == widths[0] for wd_ in widths) and sum(widths) == w_mix.shape[1]
    tile = lambda width: pl.BlockSpec((1, tm, width), lambda i, j: (i, j, 0))
    in_specs = [tile(d)] + [tile(wd_) for wd_ in widths]
    in_specs += [_layer_spec((widths[0], d), mix_layer, block=k) for k in range(n_mix)]
    in_specs += [
        _const_spec((1, d)),
        _layer_spec(wq.shape[1:], layer),
        pl.BlockSpec((None, 1, N_MEM, d), lambda i, j: (layer, i, 0, 0)),
        pl.BlockSpec((None, 1, N_MEM, d), lambda i, j: (layer, i, 0, 1)),
        _layer_spec(wo.shape[1:], layer),
        _const_spec((1, d)),
        _layer_spec(wu.shape[1:], layer),
        _layer_spec(wd.shape[1:], layer),
        _const_spec((1, d)),
    ]
    return pl.pallas_call(
        functools.partial(_post_kernel, n_mix=n_mix, final_norm=final_norm),
        grid=(b, s // tm),
        in_specs=in_specs,
        out_specs=tile(d),
        out_shape=jax.ShapeDtypeStruct((b, s, d), F32),
        compiler_params=_params("parallel", "parallel"),
        name="post",
    )(h, *ys, *([w_mix] * n_mix), gx, wq, kv, kv, wo, gm, wu, wd, gf)


def _memory_kv(mem2d, g, wkv):
    m, d = mem2d.shape
    depth, _, n = wkv.shape
    tm = ROW_TILE
    return pl.pallas_call(
        functools.partial(_norm_proj_kernel, q_cols=0),
        grid=(depth, m // tm),
        in_specs=[
            pl.BlockSpec((tm, d), lambda l, i: (i, 0)),
            _const_spec((1, d)),
            pl.BlockSpec((None, d, n), lambda l, i: (l, 0, 0)),
        ],
        out_specs=pl.BlockSpec((None, tm, n), lambda l, i: (l, i, 0)),
        out_shape=jax.ShapeDtypeStruct((depth, m, n), BF16),
        compiler_params=_params("parallel", "parallel"),
        name="memory_kv",
    )(mem2d, g, wkv)


def _block_diag(w):
    g, n, _ = w.shape
    eye = jnp.eye(g, dtype=w.dtype)
    return (eye[:, None, :, None] * w[:, :, None, :]).reshape(g * n, g * n)


def kernel(x, mem, g_mem, g_final, mix_norm_g, xattn_norm_g, mlp_norm_g, w_in_even, conv_w, conv_b, w_rgate, b_rgate, w_igate, b_igate, lru_lambda, fox_forget_b, w_out_even, w_in_odd, lambda_q1, lambda_k1, lambda_q2, lambda_k2, diff_norm_g, w_out_odd, xattn_wq, xattn_wkv, xattn_wo, w_up, w_down):
    b, s, d = x.shape
    depth = mix_norm_g.shape[0]
    m = b * s
    row = lambda v: v.reshape(1, -1).astype(F32)

    wq_all, wo_all = xattn_wq.astype(BF16), xattn_wo.astype(BF16)
    wu_all, wd_all = w_up.astype(BF16), w_down.astype(BF16)
    w_out_even_all, w_out_odd_all = w_out_even.astype(BF16), w_out_odd.astype(BF16)
    kv_all = _memory_kv(mem.reshape(b * N_MEM, d), row(g_mem), xattn_wkv.astype(BF16))
    kv_all = kv_all.reshape(depth, b, N_MEM, 2 * d)

    h = x
    for layer in range(depth):
        i = layer // 2
        if layer % 2 == 0:
            w_in = jnp.pad(w_in_even[i], ((0, 0), (0, 2688 - w_in_even.shape[2]))).astype(BF16)
            zl, qkv, kx = _even_proj(h, row(mix_norm_g[layer]), w_in,
                                     fox_forget_b[i].reshape(FOX_HEADS, 1).astype(F32))
            wg = jnp.concatenate([_block_diag(w_rgate[i]), _block_diag(w_igate[i])], axis=1).astype(BF16)
            bg = jnp.concatenate([b_rgate[i], b_igate[i]]).reshape(1, -1).astype(F32)
            y_lru = _lru(zl, conv_w[i].astype(F32), row(conv_b[i]), wg, bg, row(lru_lambda[i]))
            y_fox = _attention("fox", qkv, FOX_HEADS // 2, (kx,))
            ys, w_mix = [y_lru, y_fox], w_out_even_all
        else:
            z = _norm_proj(h.reshape(m, d), row(mix_norm_g[layer]), w_in_odd[i].astype(BF16), BF16,
                           q_cols=DIFF_HEADS * 2 * HEAD_DIM)
            lambda_init = 0.8 - 0.6 * math.exp(-0.3 * layer)
            lamv = jnp.stack([lambda_q1[i], lambda_k1[i], lambda_q2[i], lambda_k2[i]]).astype(F32)
            y = _attention("diff", z.reshape(b, s, -1), DIFF_HEADS, (lamv, row(diff_norm_g[i])),
                           lambda_init=lambda_init)
            ys, w_mix = [y], w_out_odd_all

        h = _post(h, ys, w_mix, i, row(xattn_norm_g[layer]), wq_all, kv_all, wo_all,
                  row(mlp_norm_g[layer]), wu_all, wd_all, layer,
                  row(g_final), final_norm=(layer == depth - 1))
    return h
```

```python
import functools
import math

import jax
import jax.numpy as jnp
from jax import lax
from jax.experimental import pallas as pl
from jax.experimental.pallas import tpu as pltpu

F32 = jnp.float32
BF16 = jnp.bfloat16

LANES = 128
SUBLANES = 8
VMEM_LIMIT = 56 * 1024 * 1024

EPS = 1e-6
N_MEM = 256
LRU_WIDTH = 512
LRU_BLOCKS = 8
CONV_WIDTH = 4
LRU_C = 8.0
HEAD_DIM = 64
FOX_HEADS = 8
DIFF_HEADS = 8
XATTN_HEADS = 4

LOG2E = 1.4426950408889634
QK_SCALE = HEAD_DIM ** -0.5 * LOG2E
ONES_ROWS = 16

ROW_TILE = 512
PROJ_TILE = 1024
ATTN_TILE = 512
ITEMS_PER_TRIP = 14
LRU_TILE = 256


def _params(*sem):
    return pltpu.CompilerParams(dimension_semantics=sem, vmem_limit_bytes=VMEM_LIMIT)


def _rms(x, g):
    return x * lax.rsqrt(jnp.mean(x * x, axis=-1, keepdims=True) + EPS) * g


def _dot(a, b):
    return jnp.dot(a, b, preferred_element_type=F32)


def _dot_nt(a, b):
    return lax.dot_general(a, b, (((1,), (1,)), ((), ())), preferred_element_type=F32)


def _log_sigmoid(x):
    return jnp.minimum(x, 0.0) - jnp.log1p(jnp.exp(-jnp.abs(x)))


def _sigmoid(x):
    return 0.5 * jnp.tanh(0.5 * x) + 0.5


def _expm1_given_exp(x, u):
    near = jnp.where(u == 1.0, x, (u - 1.0) * x / jnp.log(u))
    return jnp.where(jnp.abs(x) < 0.5, near, u - 1.0)


def _const_spec(shape):
    nd = len(shape)
    return pl.BlockSpec(shape, lambda *_: (0,) * nd)


def _even_proj_kernel(x_ref, g_ref, w_ref, fb_ref, zl_ref, qt_ref, k_ref, vt_ref, kx_ref, carry_ref):
    si = pl.program_id(1)
    tm = x_ref.shape[1]
    xn = _rms(x_ref[0], g_ref[...]).astype(BF16)
    fl = _dot(xn, w_ref[:, 2560:2688])
    lf = _log_sigmoid(fl.T[0:FOX_HEADS, :] + fb_ref[...])

    @pl.when(si == 0)
    def _():
        carry_ref[...] = jnp.zeros_like(carry_ref)

    pos = lax.broadcasted_iota(jnp.int32, lf.shape, 1)
    shift = 1
    while shift < tm:
        lf = lf + jnp.where(pos >= shift, pltpu.roll(lf, shift, 1), 0.0)
        shift *= 2
    cum = lf + carry_ref[:, 0:1]
    carry_ref[...] = jnp.broadcast_to(cum[:, tm - 1:tm], carry_ref.shape)

    c = cum * (-LOG2E)
    hi = c.astype(BF16).astype(F32)
    mid = (c - hi).astype(BF16).astype(F32)
    lo = (c - hi - mid).astype(BF16).astype(F32)
    pieces = jnp.concatenate([hi, mid, lo, jnp.zeros((LANES - 3 * FOX_HEADS, tm), F32)], axis=0)
    kx_ref[0] = pieces.T.astype(BF16)

    zl_ref[0] = _dot(xn, w_ref[:, 0:1024])
    _store_group_transposed(qt_ref, _dot(xn, w_ref[:, 1024:1536]) * QK_SCALE)
    _store_groups(k_ref, _dot(xn, w_ref[:, 1536:2048]))
    _store_group_transposed(vt_ref, _dot(xn, w_ref[:, 2048:2560]))


def _store_groups(ref, z):
    for grp in range(z.shape[1] // LANES):
        ref[0, grp] = z[:, grp * LANES:(grp + 1) * LANES].astype(ref.dtype)


def _store_group_transposed(ref, z):
    zt = z.T
    for grp in range(z.shape[1] // LANES):
        ref[0, grp] = zt[grp * LANES:(grp + 1) * LANES].astype(ref.dtype)


def _qkv_specs(b, s, tm, n_groups):
    specs = [
        pl.BlockSpec((1, n_groups, LANES, tm), lambda i, j: (i, 0, 0, j)),
        pl.BlockSpec((1, n_groups, tm, LANES), lambda i, j: (i, 0, j, 0)),
        pl.BlockSpec((1, n_groups, LANES, tm), lambda i, j: (i, 0, 0, j)),
    ]
    shapes = [
        jax.ShapeDtypeStruct((b, n_groups, LANES, s), BF16),
        jax.ShapeDtypeStruct((b, n_groups, s, LANES), BF16),
        jax.ShapeDtypeStruct((b, n_groups, LANES, s), BF16),
    ]
    return specs, shapes


def _even_proj(h, g, w, fb):
    b, s, d = h.shape
    tm = PROJ_TILE
    n = w.shape[1]
    qkv_specs, qkv_shapes = _qkv_specs(b, s, tm, FOX_HEADS // 2)
    return pl.pallas_call(
        _even_proj_kernel,
        grid=(b, s // tm),
        in_specs=[
            pl.BlockSpec((1, tm, d), lambda i, j: (i, j, 0)),
            _const_spec((1, d)),
            _const_spec((d, n)),
            _const_spec((FOX_HEADS, 1)),
        ],
        out_specs=[pl.BlockSpec((1, tm, 1024), lambda i, j: (i, j, 0)), *qkv_specs,
                   pl.BlockSpec((1, tm, LANES), lambda i, j: (i, j, 0))],
        out_shape=[jax.ShapeDtypeStruct((b, s, 1024), F32), *qkv_shapes,
                   jax.ShapeDtypeStruct((b, s, LANES), BF16)],
        scratch_shapes=[pltpu.VMEM((FOX_HEADS, LANES), F32)],
        compiler_params=_params("parallel", "arbitrary"),
        name="even_proj",
    )(h, g, w, fb)


def _odd_proj_kernel(x_ref, g_ref, w_ref, qt_ref, k_ref, vt_ref):
    n = w_ref.shape[1] // 3
    xn = _rms(x_ref[0], g_ref[...]).astype(BF16)
    _store_group_transposed(qt_ref, _dot(xn, w_ref[:, 0:n]) * QK_SCALE)
    _store_groups(k_ref, _dot(xn, w_ref[:, n:2 * n]))
    _store_group_transposed(vt_ref, _dot(xn, w_ref[:, 2 * n:3 * n]))


def _odd_proj(h, g, w):
    b, s, d = h.shape
    tm = PROJ_TILE
    qkv_specs, qkv_shapes = _qkv_specs(b, s, tm, DIFF_HEADS)
    return pl.pallas_call(
        _odd_proj_kernel,
        grid=(b, s // tm),
        in_specs=[pl.BlockSpec((1, tm, d), lambda i, j: (i, j, 0)), _const_spec((1, d)), _const_spec(w.shape)],
        out_specs=qkv_specs,
        out_shape=qkv_shapes,
        compiler_params=_params("parallel", "parallel"),
        name="odd_proj",
    )(h, g, w)


def _norm_proj_kernel(x_ref, g_ref, w_ref, o_ref):
    xn = _rms(x_ref[...], g_ref[...]).astype(BF16)
    o_ref[...] = _dot(xn, w_ref[...]).astype(o_ref.dtype)


def _gelu_tanh(x):
    return 0.5 * x * (1.0 + jnp.tanh(math.sqrt(2.0 / math.pi) * (x + 0.044715 * (x * x * x))))


def _lru_kernel(xb_ref, gb_ref, cw_ref, cb_ref, wg_ref, bg_ref, lam_ref, y_ref,
                xbuf, a_s, u_s, hc):
    ti = pl.program_id(1)
    ts = xb_ref.shape[1]
    w = LRU_WIDTH

    @pl.when(ti == 0)
    def _():
        xbuf[0:SUBLANES, :] = jnp.zeros((SUBLANES, w), F32)
        hc[...] = jnp.zeros_like(hc)

    xbuf[SUBLANES:SUBLANES + ts, :] = xb_ref[0]
    xc = cb_ref[...]
    for k in range(CONV_WIDTH):
        lag = CONV_WIDTH - 1 - k
        xc = xc + xbuf[SUBLANES - lag:SUBLANES - lag + ts, :] * cw_ref[k:k + 1, :]
    xbuf[0:SUBLANES, :] = xbuf[ts:ts + SUBLANES, :]

    gates = _dot(xc.astype(BF16), wg_ref[...]) + bg_ref[...]
    r = _sigmoid(gates[:, 0:w])
    i = _sigmoid(gates[:, w:2 * w])
    log_a = LRU_C * r * _log_sigmoid(lam_ref[...])
    a = jnp.exp(log_a)
    a_s[...] = a
    v = -_expm1_given_exp(2.0 * log_a, a * a)
    sqrt_v = jnp.where(v > 0.0, v * lax.rsqrt(v), 0.0)
    u_s[...] = sqrt_v * (i * xc)

    row = lax.broadcasted_iota(jnp.int32, (SUBLANES, w), 0)

    def group(gidx, carry):
        off = pl.multiple_of(gidx * SUBLANES, SUBLANES)
        a = a_s[pl.ds(off, SUBLANES), :]
        u = u_s[pl.ds(off, SUBLANES), :]
        for sh in (1, 2, 4):
            keep = row >= sh
            u = u + a * jnp.where(keep, pltpu.roll(u, sh, 0), 0.0)
            a = a * jnp.where(keep, pltpu.roll(a, sh, 0), 1.0)
        hgrp = u + a * carry
        u_s[pl.ds(off, SUBLANES), :] = hgrp
        return jnp.broadcast_to(hgrp[SUBLANES - 1:SUBLANES, :], (SUBLANES, w))

    hc[...] = lax.fori_loop(0, ts // SUBLANES, group, hc[...], unroll=4)
    y_ref[0] = (u_s[...] * _gelu_tanh(gb_ref[0])).astype(y_ref.dtype)


def _lru(zl, cw, cb, wg, bg, lam):
    b, s, _ = zl.shape
    ts = LRU_TILE
    w = LRU_WIDTH
    return pl.pallas_call(
        _lru_kernel,
        grid=(b, s // ts),
        in_specs=[
            pl.BlockSpec((1, ts, w), lambda i, j: (i, j, 0)),
            pl.BlockSpec((1, ts, w), lambda i, j: (i, j, 1)),
            _const_spec((CONV_WIDTH, w)),
            _const_spec((1, w)),
            _const_spec((w, 2 * w)),
            _const_spec((1, 2 * w)),
            _const_spec((1, w)),
        ],
        out_specs=pl.BlockSpec((1, ts, w), lambda i, j: (i, j, 0)),
        out_shape=jax.ShapeDtypeStruct((b, s, w), BF16),
        scratch_shapes=[
            pltpu.VMEM((SUBLANES + ts, w), F32),
            pltpu.VMEM((ts, w), F32),
            pltpu.VMEM((ts, w), F32),
            pltpu.VMEM((SUBLANES, w), F32),
        ],
        compiler_params=_params("parallel", "arbitrary"),
        name="rglru",
    )(zl, zl, cw, cb, wg, bg, lam)


def _attn_items(nt):
    items = [(qi, j) for qi in range(1, nt) for j in range(qi)]
    return items + [items[-1]] * 2


def _attn_kernel(*refs, mode, lambda_init):
    if mode == "fox":
        tab_ref, qt_ref, k_ref, vt_ref, kx_ref, o_ref, pk_s, sa0, sa1, sb0, sb1, mc_s, m_s, acc_s = refs
    else:
        tab_ref, qt_ref, k_ref, vt_ref, lamv_ref, g_ref, o_ref, sa0, sa1, sb0, sb1, mc_s, m_s, acc_s = refs
    bufs = ((sa0, sa1), (sb0, sb1))
    hp = pl.program_id(1)
    t = ATTN_TILE
    s_len = k_ref.shape[2]
    nt = s_len // t
    dv = HEAD_DIM if mode == "fox" else LANES

    if mode == "fox":
        rowi = lax.broadcasted_iota(jnp.int32, (LANES, t), 0)
        for e in range(2):
            head = 2 * hp + e
            pick = (rowi == head) | (rowi == head + FOX_HEADS) | (rowi == head + 2 * FOX_HEADS)
            pk_s[e] = jnp.where(pick, 1.0, 0.0).astype(BF16)

    def q_weights(qi, e):
        qt = qt_ref[0, 0, :, pl.ds(pl.multiple_of(qi * t, t), t)]
        zero = jnp.zeros((HEAD_DIM, t), BF16)
        parts = [qt[0:HEAD_DIM], zero] if e == 0 else [zero, qt[HEAD_DIM:LANES]]
        if mode == "fox":
            parts.append(pk_s[e])
        return jnp.concatenate(parts, axis=0)

    def v_rows(e, off):
        rows = slice(e * dv, (e + 1) * dv) if mode == "fox" else slice(0, dv)
        return jnp.concatenate([vt_ref[0, 0, rows, pl.ds(off, t)], jnp.ones((ONES_ROWS, t), BF16)], axis=0)

    def colmax(x):
        acc = x[0:16]
        for i in range(1, x.shape[0] // 16):
            acc = jnp.maximum(acc, x[i * 16:(i + 1) * 16])
        return jnp.max(acc, axis=0, keepdims=True)

    def key_block(off):
        kb = k_ref[0, 0, pl.ds(off, t), :]
        if mode == "fox":
            kb = jnp.concatenate([kb, kx_ref[0, pl.ds(off, t), :]], axis=1)
        return kb

    def stage1(qi, j, par, e):
        s = _dot(key_block(pl.multiple_of(j * t, t)), q_weights(qi, e))
        bufs[par][e][...] = s
        mc_s[par, e] = colmax(s)

    def stage2(qi, j, par, e):
        off = pl.multiple_of(j * t, t)
        vt = v_rows(e, off)
        m_prev = m_s[qi, e]
        m_new = jnp.maximum(m_prev, mc_s[par, e])
        alpha = jnp.exp2(m_prev - m_new)
        p = jnp.exp2(bufs[par][e][...] - m_new).astype(BF16)
        acc_s[qi, e] = alpha * acc_s[qi, e] + _dot(vt, p)
        m_s[qi, e] = m_new

    half = t // 2

    def stage1_diag(qi, par, e):
        kb = key_block(qi * t)
        w = q_weights(qi, e)
        keys = lax.broadcasted_iota(jnp.int32, (half, half), 0)
        qpos = lax.broadcasted_iota(jnp.int32, (half, half), 1)
        tri = keys <= qpos
        s_left = jnp.where(tri, _dot(kb[0:half], w[:, 0:half]), -jnp.inf)
        s_right = _dot(kb, w[:, half:t])
        s_top = s_right[0:half]
        s_bot = jnp.where(tri, s_right[half:t], -jnp.inf)
        buf = bufs[par][e]
        buf[0:half, 0:half] = s_left
        buf[0:half, half:t] = s_top
        buf[half:t, half:t] = s_bot
        mc_s[par, e] = jnp.concatenate(
            [colmax(s_left), jnp.maximum(colmax(s_top), colmax(s_bot))], axis=1)

    def stage2_diag(qi, par, e):
        buf = bufs[par][e]
        vt = v_rows(e, qi * t)
        m_prev = m_s[qi, e]
        m_new = jnp.maximum(m_prev, mc_s[par, e])
        alpha = jnp.exp2(m_prev - m_new)
        p_left = jnp.exp2(buf[0:half, 0:half] - m_new[:, 0:half]).astype(BF16)
        p_right = jnp.exp2(buf[:, half:t] - m_new[:, half:t]).astype(BF16)
        upd = jnp.concatenate([_dot(vt[:, 0:half], p_left), _dot(vt, p_right)], axis=1)
        acc_s[qi, e] = alpha * acc_s[qi, e] + upd
        m_s[qi, e] = m_new

    def item(w):
        return tab_ref[w, 0], tab_ref[w, 1]

    m_s[...] = jnp.full(m_s.shape, -jnp.inf, F32)
    acc_s[...] = jnp.zeros(acc_s.shape, F32)

    qa, ja = item(0)
    stage1(qa, ja, 0, 0)
    stage1(qa, ja, 0, 1)

    def body(i, c):
        w = ITEMS_PER_TRIP * i
        for k in range(ITEMS_PER_TRIP):
            qc, jc = item(w + k)
            qn, jn = item(w + k + 1)
            for e in range(2):
                stage1(qn, jn, (k + 1) % 2, e)
                stage2(qc, jc, k % 2, e)
        return c

    lax.fori_loop(0, (nt * (nt - 1)) // (2 * ITEMS_PER_TRIP), body, 0)

    stage1_diag(0, 0, 0)
    stage1_diag(0, 0, 1)
    for qi in range(nt):
        par = qi % 2
        for e in range(2):
            if qi + 1 < nt:
                stage1_diag(qi + 1, 1 - par, e)
            stage2_diag(qi, par, e)

    for qi in range(nt):
        o0 = acc_s[qi, 0, 0:dv] / acc_s[qi, 0, dv:dv + 1]
        o1 = acc_s[qi, 1, 0:dv] / acc_s[qi, 1, dv:dv + 1]
        if mode == "fox":
            o = jnp.concatenate([o0, o1], axis=0).T
        else:
            lv = lamv_ref[...]
            lam = (jnp.exp(jnp.sum(lv[0:1] * lv[1:2], axis=1, keepdims=True))
                   - jnp.exp(jnp.sum(lv[2:3] * lv[3:4], axis=1, keepdims=True)) + lambda_init)
            o = (o0 - lam * o1).T
            o = o * lax.rsqrt(jnp.mean(o * o, axis=-1, keepdims=True) + EPS)
            o = o * g_ref[...] * (1.0 - lambda_init)
        o_ref[0, qi * t:(qi + 1) * t, :] = o.astype(o_ref.dtype)


def _attention(mode, qt, k, vt, extra, lambda_init=0.0):
    b, n_groups, _, s = qt.shape
    t = ATTN_TILE
    nt = s // t
    assert (nt * (nt - 1)) % (2 * ITEMS_PER_TRIP) == 0
    tab = jnp.asarray(_attn_items(nt), jnp.int32)
    in_specs = [
        pl.BlockSpec(memory_space=pltpu.SMEM),
        pl.BlockSpec((1, 1, LANES, s), lambda i, h: (i, h, 0, 0)),
        pl.BlockSpec((1, 1, s, LANES), lambda i, h: (i, h, 0, 0)),
        pl.BlockSpec((1, 1, LANES, s), lambda i, h: (i, h, 0, 0)),
    ]
    if mode == "fox":
        (kx,) = extra
        in_specs.append(pl.BlockSpec((1, s, LANES), lambda i, h: (i, 0, 0)))
        dv = HEAD_DIM
        pick_scratch = [pltpu.VMEM((2, LANES, t), BF16)]
    else:
        lamv, g = extra
        in_specs += [_const_spec(lamv.shape), _const_spec(g.shape)]
        dv = LANES
        pick_scratch = []
    return pl.pallas_call(
        functools.partial(_attn_kernel, mode=mode, lambda_init=lambda_init),
        grid=(b, n_groups),
        in_specs=in_specs,
        out_specs=pl.BlockSpec((1, s, LANES), lambda i, h: (i, 0, h)),
        out_shape=jax.ShapeDtypeStruct((b, s, n_groups * LANES), BF16),
        scratch_shapes=[
            *pick_scratch,
            pltpu.VMEM((t, t), F32),
            pltpu.VMEM((t, t), F32),
            pltpu.VMEM((t, t), F32),
            pltpu.VMEM((t, t), F32),
            pltpu.VMEM((2, 2, 1, t), F32),
            pltpu.VMEM((nt, 2, 1, t), F32),
            pltpu.VMEM((nt, 2, dv + ONES_ROWS, t), F32),
        ],
        compiler_params=_params("parallel", "parallel"),
        name=mode + "_attn",
    )(tab, qt, k, vt, *extra)


FF_CHUNK = 1024


def _post_kernel(*refs, n_mix, final_norm):
    h_ref = refs[0]
    ys = refs[1:1 + n_mix]
    ws = refs[1 + n_mix:1 + 2 * n_mix]
    gx_ref, wq_ref, k_ref, v_ref, wo_ref, gm_ref, wu_ref, wd_ref, gf_ref, o_ref = refs[1 + 2 * n_mix:]
    d = h_ref.shape[2]
    dh = d // XATTN_HEADS

    h = h_ref[0]
    for y_ref, w_ref in zip(ys, ws):
        h = h + _dot(y_ref[0], w_ref[...])

    q = _dot(_rms(h, gx_ref[...]).astype(BF16), wq_ref[...]).astype(BF16)
    outs = []
    for e in range(XATTN_HEADS):
        sl = slice(e * dh, (e + 1) * dh)
        s = _dot_nt(q[:, sl], k_ref[0, :, sl]) * (dh ** -0.5)
        p = jnp.exp(s - jnp.max(s, axis=1, keepdims=True))
        p = p / jnp.sum(p, axis=1, keepdims=True)
        outs.append(_dot(p.astype(BF16), v_ref[0, :, sl]).astype(BF16))
    h = h + _dot(jnp.concatenate(outs, axis=1), wo_ref[...])

    xn = _rms(h, gm_ref[...]).astype(BF16)
    for c in range(wu_ref.shape[1] // FF_CHUNK):
        sl = slice(c * FF_CHUNK, (c + 1) * FF_CHUNK)
        u = jnp.maximum(_dot(xn, wu_ref[:, sl]), 0.0)
        h = h + _dot((u * u).astype(BF16), wd_ref[sl, :])
    if final_norm:
        h = _rms(h, gf_ref[...])
    o_ref[0] = h


def _layer_spec(shape, layer, block=0):
    return pl.BlockSpec((None,) + tuple(shape), lambda *_: (layer, block, 0))


def _post(h, ys, w_mix, mix_layer, gx, wq, kv, wo, gm, wu, wd, layer, gf, final_norm):
    b, s, d = h.shape
    tm = ROW_TILE
    n_mix = len(ys)
    widths = [y.shape[2] for y in ys]
    assert all(wd_ == widths[0] for wd_ in widths) and sum(widths) == w_mix.shape[1]
    tile = lambda width: pl.BlockSpec((1, tm, width), lambda i, j: (i, j, 0))
    in_specs = [tile(d)] + [tile(wd_) for wd_ in widths]
    in_specs += [_layer_spec((widths[0], d), mix_layer, block=k) for k in range(n_mix)]
    in_specs += [
        _const_spec((1, d)),
        _layer_spec(wq.shape[1:], layer),
        pl.BlockSpec((None, 1, N_MEM, d), lambda i, j: (layer, i, 0, 0)),
        pl.BlockSpec((None, 1, N_MEM, d), lambda i, j: (layer, i, 0, 1)),
        _layer_spec(wo.shape[1:], layer),
        _const_spec((1, d)),
        _layer_spec(wu.shape[1:], layer),
        _layer_spec(wd.shape[1:], layer),
        _const_spec((1, d)),
    ]
    return pl.pallas_call(
        functools.partial(_post_kernel, n_mix=n_mix, final_norm=final_norm),
        grid=(b, s // tm),
        in_specs=in_specs,
        out_specs=tile(d),
        out_shape=jax.ShapeDtypeStruct((b, s, d), F32),
        compiler_params=_params("parallel", "parallel"),
        name="post",
    )(h, *ys, *([w_mix] * n_mix), gx, wq, kv, kv, wo, gm, wu, wd, gf)


def _memory_kv(mem2d, g, wkv):
    m, d = mem2d.shape
    depth, _, n = wkv.shape
    tm = ROW_TILE
    return pl.pallas_call(
        _norm_proj_kernel,
        grid=(depth, m // tm),
        in_specs=[
            pl.BlockSpec((tm, d), lambda l, i: (i, 0)),
            _const_spec((1, d)),
            pl.BlockSpec((None, d, n), lambda l, i: (l, 0, 0)),
        ],
        out_specs=pl.BlockSpec((None, tm, n), lambda l, i: (l, i, 0)),
        out_shape=jax.ShapeDtypeStruct((depth, m, n), BF16),
        compiler_params=_params("parallel", "parallel"),
        name="memory_kv",
    )(mem2d, g, wkv)


def _block_diag(w):
    g, n, _ = w.shape
    eye = jnp.eye(g, dtype=w.dtype)
    return (eye[:, None, :, None] * w[:, :, None, :]).reshape(g * n, g * n)


def kernel(x, mem, g_mem, g_final, mix_norm_g, xattn_norm_g, mlp_norm_g, w_in_even, conv_w, conv_b, w_rgate, b_rgate, w_igate, b_igate, lru_lambda, fox_forget_b, w_out_even, w_in_odd, lambda_q1, lambda_k1, lambda_q2, lambda_k2, diff_norm_g, w_out_odd, xattn_wq, xattn_wkv, xattn_wo, w_up, w_down):
    b, s, d = x.shape
    depth = mix_norm_g.shape[0]
    m = b * s
    row = lambda v: v.reshape(1, -1).astype(F32)

    wq_all, wo_all = xattn_wq.astype(BF16), xattn_wo.astype(BF16)
    wu_all, wd_all = w_up.astype(BF16), w_down.astype(BF16)
    w_out_even_all, w_out_odd_all = w_out_even.astype(BF16), w_out_odd.astype(BF16)
    kv_all = _memory_kv(mem.reshape(b * N_MEM, d), row(g_mem), xattn_wkv.astype(BF16))
    kv_all = kv_all.reshape(depth, b, N_MEM, 2 * d)

    h = x
    for layer in range(depth):
        i = layer // 2
        if layer % 2 == 0:
            w_in = jnp.pad(w_in_even[i], ((0, 0), (0, 2688 - w_in_even.shape[2]))).astype(BF16)
            zl, qt, k, vt, kx = _even_proj(h, row(mix_norm_g[layer]), w_in,
                                           fox_forget_b[i].reshape(FOX_HEADS, 1).astype(F32))
            wg = jnp.concatenate([_block_diag(w_rgate[i]), _block_diag(w_igate[i])], axis=1).astype(BF16)
            bg = jnp.concatenate([b_rgate[i], b_igate[i]]).reshape(1, -1).astype(F32)
            y_lru = _lru(zl, conv_w[i].astype(F32), row(conv_b[i]), wg, bg, row(lru_lambda[i]))
            y_fox = _attention("fox", qt, k, vt, (kx,))
            ys, w_mix = [y_lru, y_fox], w_out_even_all
        else:
            qt, k, vt = _odd_proj(h, row(mix_norm_g[layer]), w_in_odd[i].astype(BF16))
            lambda_init = 0.8 - 0.6 * math.exp(-0.3 * layer)
            lamv = jnp.stack([lambda_q1[i], lambda_k1[i], lambda_q2[i], lambda_k2[i]]).astype(F32)
            y = _attention("diff", qt, k, vt, (lamv, row(diff_norm_g[i])), lambda_init=lambda_init)
            ys, w_mix = [y], w_out_odd_all

        h = _post(h, ys, w_mix, i, row(xattn_norm_g[layer]), wq_all, kv_all, wo_all,
                  row(mlp_norm_g[layer]), wu_all, wd_all, layer,
                  row(g_final), final_norm=(layer == depth - 1))
    return h
```

```python
import functools
import math

import jax
import jax.numpy as jnp
from jax import lax
from jax.experimental import pallas as pl
from jax.experimental.pallas import tpu as pltpu

F32 = jnp.float32
BF16 = jnp.bfloat16

LANES = 128
SUBLANES = 8
VMEM_LIMIT = 56 * 1024 * 1024

EPS = 1e-6
N_MEM = 256
LRU_WIDTH = 512
LRU_BLOCKS = 8
CONV_WIDTH = 4
LRU_C = 8.0
HEAD_DIM = 64
FOX_HEADS = 8
DIFF_HEADS = 8
XATTN_HEADS = 4

LOG2E = 1.4426950408889634
QK_SCALE = HEAD_DIM ** -0.5 * LOG2E
ONES_ROWS = 16

ROW_TILE = 512
PROJ_TILE = 1024
ATTN_TILE = 512
ITEMS_PER_TRIP = 14
LRU_TILE = 1024


def _params(*sem):
    return pltpu.CompilerParams(dimension_semantics=sem, vmem_limit_bytes=VMEM_LIMIT)


def _rms(x, g):
    return x * lax.rsqrt(jnp.mean(x * x, axis=-1, keepdims=True) + EPS) * g


def _dot(a, b):
    return jnp.dot(a, b, preferred_element_type=F32)


def _dot_nt(a, b):
    return lax.dot_general(a, b, (((1,), (1,)), ((), ())), preferred_element_type=F32)


def _log_sigmoid(x):
    return jnp.minimum(x, 0.0) - jnp.log1p(jnp.exp(-jnp.abs(x)))


def _sigmoid(x):
    return 0.5 * jnp.tanh(0.5 * x) + 0.5


def _expm1_given_exp(x, u):
    near = jnp.where(u == 1.0, x, (u - 1.0) * x / jnp.log(u))
    return jnp.where(jnp.abs(x) < 0.5, near, u - 1.0)


def _const_spec(shape):
    nd = len(shape)
    return pl.BlockSpec(shape, lambda *_: (0,) * nd)


def _even_proj_kernel(x_ref, g_ref, w_ref, fb_ref, zl_ref, qt_ref, k_ref, vt_ref, kx_ref, carry_ref):
    si = pl.program_id(1)
    tm = x_ref.shape[1]
    xn = _rms(x_ref[0], g_ref[...]).astype(BF16)
    fl = _dot(xn, w_ref[:, 2560:2688])
    lf = _log_sigmoid(fl.T[0:FOX_HEADS, :] + fb_ref[...])

    @pl.when(si == 0)
    def _():
        carry_ref[...] = jnp.zeros_like(carry_ref)

    pos = lax.broadcasted_iota(jnp.int32, lf.shape, 1)
    shift = 1
    while shift < tm:
        lf = lf + jnp.where(pos >= shift, pltpu.roll(lf, shift, 1), 0.0)
        shift *= 2
    cum = lf + carry_ref[:, 0:1]
    carry_ref[...] = jnp.broadcast_to(cum[:, tm - 1:tm], carry_ref.shape)

    c = cum * (-LOG2E)
    hi = c.astype(BF16).astype(F32)
    mid = (c - hi).astype(BF16).astype(F32)
    lo = (c - hi - mid).astype(BF16).astype(F32)
    pieces = jnp.concatenate([hi, mid, lo, jnp.zeros((LANES - 3 * FOX_HEADS, tm), F32)], axis=0)
    kx_ref[0] = pieces.T.astype(BF16)

    zl_ref[0] = _dot(xn, w_ref[:, 0:1024])
    _store_group_transposed(qt_ref, _dot(xn, w_ref[:, 1024:1536]) * QK_SCALE)
    _store_groups(k_ref, _dot(xn, w_ref[:, 1536:2048]))
    _store_group_transposed(vt_ref, _dot(xn, w_ref[:, 2048:2560]))


def _store_groups(ref, z):
    for grp in range(z.shape[1] // LANES):
        ref[0, grp] = z[:, grp * LANES:(grp + 1) * LANES].astype(ref.dtype)


def _store_group_transposed(ref, z):
    zt = z.T
    for grp in range(z.shape[1] // LANES):
        ref[0, grp] = zt[grp * LANES:(grp + 1) * LANES].astype(ref.dtype)


def _qkv_specs(b, s, tm, n_groups):
    specs = [
        pl.BlockSpec((1, n_groups, LANES, tm), lambda i, j: (i, 0, 0, j)),
        pl.BlockSpec((1, n_groups, tm, LANES), lambda i, j: (i, 0, j, 0)),
        pl.BlockSpec((1, n_groups, LANES, tm), lambda i, j: (i, 0, 0, j)),
    ]
    shapes = [
        jax.ShapeDtypeStruct((b, n_groups, LANES, s), BF16),
        jax.ShapeDtypeStruct((b, n_groups, s, LANES), BF16),
        jax.ShapeDtypeStruct((b, n_groups, LANES, s), BF16),
    ]
    return specs, shapes


def _even_proj(h, g, w, fb):
    b, s, d = h.shape
    tm = PROJ_TILE
    n = w.shape[1]
    qkv_specs, qkv_shapes = _qkv_specs(b, s, tm, FOX_HEADS // 2)
    return pl.pallas_call(
        _even_proj_kernel,
        grid=(b, s // tm),
        in_specs=[
            pl.BlockSpec((1, tm, d), lambda i, j: (i, j, 0)),
            _const_spec((1, d)),
            _const_spec((d, n)),
            _const_spec((FOX_HEADS, 1)),
        ],
        out_specs=[pl.BlockSpec((1, tm, 1024), lambda i, j: (i, j, 0)), *qkv_specs,
                   pl.BlockSpec((1, tm, LANES), lambda i, j: (i, j, 0))],
        out_shape=[jax.ShapeDtypeStruct((b, s, 1024), F32), *qkv_shapes,
                   jax.ShapeDtypeStruct((b, s, LANES), BF16)],
        scratch_shapes=[pltpu.VMEM((FOX_HEADS, LANES), F32)],
        compiler_params=_params("parallel", "arbitrary"),
        name="even_proj",
    )(h, g, w, fb)


def _odd_proj_rows(h, g_ref, w_ref, qt_ref, k_ref, vt_ref):
    n = w_ref.shape[1] // 3
    xn = _rms(h, g_ref[...]).astype(BF16)
    _store_group_transposed(qt_ref, _dot(xn, w_ref[:, 0:n]) * QK_SCALE)
    _store_groups(k_ref, _dot(xn, w_ref[:, n:2 * n]))
    _store_group_transposed(vt_ref, _dot(xn, w_ref[:, 2 * n:3 * n]))


def _norm_proj_kernel(x_ref, g_ref, w_ref, o_ref):
    xn = _rms(x_ref[...], g_ref[...]).astype(BF16)
    o_ref[...] = _dot(xn, w_ref[...]).astype(o_ref.dtype)


def _gelu_tanh(x):
    return 0.5 * x * (1.0 + jnp.tanh(math.sqrt(2.0 / math.pi) * (x + 0.044715 * (x * x * x))))


def _lru_kernel(xb_ref, gb_ref, cw_ref, cb_ref, wg_ref, bg_ref, lam_ref, y_ref,
                xbuf, a_s, u_s, hc):
    ti = pl.program_id(1)
    ts = xb_ref.shape[1]
    w = LRU_WIDTH

    @pl.when(ti == 0)
    def _():
        xbuf[0:SUBLANES, :] = jnp.zeros((SUBLANES, w), F32)
        hc[...] = jnp.zeros_like(hc)

    xbuf[SUBLANES:SUBLANES + ts, :] = xb_ref[0]
    xc = cb_ref[...]
    for k in range(CONV_WIDTH):
        lag = CONV_WIDTH - 1 - k
        xc = xc + xbuf[SUBLANES - lag:SUBLANES - lag + ts, :] * cw_ref[k:k + 1, :]
    xbuf[0:SUBLANES, :] = xbuf[ts:ts + SUBLANES, :]

    gates = _dot(xc.astype(BF16), wg_ref[...]) + bg_ref[...]
    r = _sigmoid(gates[:, 0:w])
    i = _sigmoid(gates[:, w:2 * w])
    log_a = LRU_C * r * _log_sigmoid(lam_ref[...])
    a = jnp.exp(log_a)
    a_s[...] = a
    v = -_expm1_given_exp(2.0 * log_a, a * a)
    sqrt_v = jnp.where(v > 0.0, v * lax.rsqrt(v), 0.0)
    u_s[...] = sqrt_v * (i * xc)

    row = lax.broadcasted_iota(jnp.int32, (SUBLANES, w), 0)

    def group(gidx, carry):
        off = pl.multiple_of(gidx * SUBLANES, SUBLANES)
        a = a_s[pl.ds(off, SUBLANES), :]
        u = u_s[pl.ds(off, SUBLANES), :]
        for sh in (1, 2, 4):
            keep = row >= sh
            u = u + a * jnp.where(keep, pltpu.roll(u, sh, 0), 0.0)
            a = a * jnp.where(keep, pltpu.roll(a, sh, 0), 1.0)
        hgrp = u + a * carry
        u_s[pl.ds(off, SUBLANES), :] = hgrp
        return jnp.broadcast_to(hgrp[SUBLANES - 1:SUBLANES, :], (SUBLANES, w))

    hc[...] = lax.fori_loop(0, ts // SUBLANES, group, hc[...], unroll=8)
    y_ref[0] = (u_s[...] * _gelu_tanh(gb_ref[0])).astype(y_ref.dtype)


def _lru(zl, cw, cb, wg, bg, lam):
    b, s, _ = zl.shape
    ts = LRU_TILE
    w = LRU_WIDTH
    return pl.pallas_call(
        _lru_kernel,
        grid=(b, s // ts),
        in_specs=[
            pl.BlockSpec((1, ts, w), lambda i, j: (i, j, 0)),
            pl.BlockSpec((1, ts, w), lambda i, j: (i, j, 1)),
            _const_spec((CONV_WIDTH, w)),
            _const_spec((1, w)),
            _const_spec((w, 2 * w)),
            _const_spec((1, 2 * w)),
            _const_spec((1, w)),
        ],
        out_specs=pl.BlockSpec((1, ts, w), lambda i, j: (i, j, 0)),
        out_shape=jax.ShapeDtypeStruct((b, s, w), BF16),
        scratch_shapes=[
            pltpu.VMEM((SUBLANES + ts, w), F32),
            pltpu.VMEM((ts, w), F32),
            pltpu.VMEM((ts, w), F32),
            pltpu.VMEM((SUBLANES, w), F32),
        ],
        compiler_params=_params("parallel", "arbitrary"),
        name="rglru",
    )(zl, zl, cw, cb, wg, bg, lam)


def _attn_items(nt):
    items = [(qi, j) for qi in range(1, nt) for j in range(qi)]
    return items + [items[-1]] * 2


def _attn_kernel(*refs, mode, lambda_init):
    if mode == "fox":
        tab_ref, qt_ref, k_ref, vt_ref, kx_ref, o_ref, pk_s, sa0, sa1, sb0, sb1, mc_s, m_s, acc_s = refs
    else:
        tab_ref, qt_ref, k_ref, vt_ref, lamv_ref, g_ref, o_ref, sa0, sa1, sb0, sb1, mc_s, m_s, acc_s = refs
    bufs = ((sa0, sa1), (sb0, sb1))
    hp = pl.program_id(1)
    t = ATTN_TILE
    s_len = k_ref.shape[2]
    nt = s_len // t
    dv = HEAD_DIM if mode == "fox" else LANES

    if mode == "fox":
        rowi = lax.broadcasted_iota(jnp.int32, (LANES, t), 0)
        for e in range(2):
            head = 2 * hp + e
            pick = (rowi == head) | (rowi == head + FOX_HEADS) | (rowi == head + 2 * FOX_HEADS)
            pk_s[e] = jnp.where(pick, 1.0, 0.0).astype(BF16)

    def q_weights(qi, e):
        qt = qt_ref[0, 0, :, pl.ds(pl.multiple_of(qi * t, t), t)]
        zero = jnp.zeros((HEAD_DIM, t), BF16)
        parts = [qt[0:HEAD_DIM], zero] if e == 0 else [zero, qt[HEAD_DIM:LANES]]
        if mode == "fox":
            parts.append(pk_s[e])
        return jnp.concatenate(parts, axis=0)

    def v_rows(e, off):
        rows = slice(e * dv, (e + 1) * dv) if mode == "fox" else slice(0, dv)
        return jnp.concatenate([vt_ref[0, 0, rows, pl.ds(off, t)], jnp.ones((ONES_ROWS, t), BF16)], axis=0)

    def colmax(x):
        acc = x[0:16]
        for i in range(1, x.shape[0] // 16):
            acc = jnp.maximum(acc, x[i * 16:(i + 1) * 16])
        return jnp.max(acc, axis=0, keepdims=True)

    def key_block(off):
        kb = k_ref[0, 0, pl.ds(off, t), :]
        if mode == "fox":
            kb = jnp.concatenate([kb, kx_ref[0, pl.ds(off, t), :]], axis=1)
        return kb

    def stage1(qi, j, par, e):
        s = _dot(key_block(pl.multiple_of(j * t, t)), q_weights(qi, e))
        bufs[par][e][...] = s
        mc_s[par, e] = colmax(s)

    def stage2(qi, j, par, e):
        off = pl.multiple_of(j * t, t)
        vt = v_rows(e, off)
        m_prev = m_s[qi, e]
        m_new = jnp.maximum(m_prev, mc_s[par, e])
        alpha = jnp.exp2(m_prev - m_new)
        p = jnp.exp2(bufs[par][e][...] - m_new).astype(BF16)
        acc_s[qi, e] = alpha * acc_s[qi, e] + _dot(vt, p)
        m_s[qi, e] = m_new

    half = t // 2

    def stage1_diag(qi, par, e):
        kb = key_block(qi * t)
        w = q_weights(qi, e)
        keys = lax.broadcasted_iota(jnp.int32, (half, half), 0)
        qpos = lax.broadcasted_iota(jnp.int32, (half, half), 1)
        tri = keys <= qpos
        s_left = jnp.where(tri, _dot(kb[0:half], w[:, 0:half]), -jnp.inf)
        s_right = _dot(kb, w[:, half:t])
        s_top = s_right[0:half]
        s_bot = jnp.where(tri, s_right[half:t], -jnp.inf)
        buf = bufs[par][e]
        buf[0:half, 0:half] = s_left
        buf[0:half, half:t] = s_top
        buf[half:t, half:t] = s_bot
        mc_s[par, e] = jnp.concatenate(
            [colmax(s_left), jnp.maximum(colmax(s_top), colmax(s_bot))], axis=1)

    def stage2_diag(qi, par, e):
        buf = bufs[par][e]
        vt = v_rows(e, qi * t)
        m_prev = m_s[qi, e]
        m_new = jnp.maximum(m_prev, mc_s[par, e])
        alpha = jnp.exp2(m_prev - m_new)
        p_left = jnp.exp2(buf[0:half, 0:half] - m_new[:, 0:half]).astype(BF16)
        p_right = jnp.exp2(buf[:, half:t] - m_new[:, half:t]).astype(BF16)
        upd = jnp.concatenate([_dot(vt[:, 0:half], p_left), _dot(vt, p_right)], axis=1)
        acc_s[qi, e] = alpha * acc_s[qi, e] + upd
        m_s[qi, e] = m_new

    def item(w):
        return tab_ref[w, 0], tab_ref[w, 1]

    m_s[...] = jnp.full(m_s.shape, -jnp.inf, F32)
    acc_s[...] = jnp.zeros(acc_s.shape, F32)

    qa, ja = item(0)
    stage1(qa, ja, 0, 0)
    stage1(qa, ja, 0, 1)

    def body(i, c):
        w = ITEMS_PER_TRIP * i
        for k in range(ITEMS_PER_TRIP):
            qc, jc = item(w + k)
            qn, jn = item(w + k + 1)
            for e in range(2):
                stage1(qn, jn, (k + 1) % 2, e)
                stage2(qc, jc, k % 2, e)
        return c

    lax.fori_loop(0, (nt * (nt - 1)) // (2 * ITEMS_PER_TRIP), body, 0)

    stage1_diag(0, 0, 0)
    stage1_diag(0, 0, 1)
    for qi in range(nt):
        par = qi % 2
        for e in range(2):
            if qi + 1 < nt:
                stage1_diag(qi + 1, 1 - par, e)
            stage2_diag(qi, par, e)

    for qi in range(nt):
        o0 = acc_s[qi, 0, 0:dv] / acc_s[qi, 0, dv:dv + 1]
        o1 = acc_s[qi, 1, 0:dv] / acc_s[qi, 1, dv:dv + 1]
        if mode == "fox":
            o = jnp.concatenate([o0, o1], axis=0).T
        else:
            lv = lamv_ref[...]
            lam = (jnp.exp(jnp.sum(lv[0:1] * lv[1:2], axis=1, keepdims=True))
                   - jnp.exp(jnp.sum(lv[2:3] * lv[3:4], axis=1, keepdims=True)) + lambda_init)
            o = (o0 - lam * o1).T
            o = o * lax.rsqrt(jnp.mean(o * o, axis=-1, keepdims=True) + EPS)
            o = o * g_ref[...] * (1.0 - lambda_init)
        o_ref[0, qi * t:(qi + 1) * t, :] = o.astype(o_ref.dtype)


def _attention(mode, qt, k, vt, extra, lambda_init=0.0):
    b, n_groups, _, s = qt.shape
    t = ATTN_TILE
    nt = s // t
    assert (nt * (nt - 1)) % (2 * ITEMS_PER_TRIP) == 0
    tab = jnp.asarray(_attn_items(nt), jnp.int32)
    in_specs = [
        pl.BlockSpec(memory_space=pltpu.SMEM),
        pl.BlockSpec((1, 1, LANES, s), lambda i, h: (i, h, 0, 0)),
        pl.BlockSpec((1, 1, s, LANES), lambda i, h: (i, h, 0, 0)),
        pl.BlockSpec((1, 1, LANES, s), lambda i, h: (i, h, 0, 0)),
    ]
    if mode == "fox":
        (kx,) = extra
        in_specs.append(pl.BlockSpec((1, s, LANES), lambda i, h: (i, 0, 0)))
        dv = HEAD_DIM
        pick_scratch = [pltpu.VMEM((2, LANES, t), BF16)]
    else:
        lamv, g = extra
        in_specs += [_const_spec(lamv.shape), _const_spec(g.shape)]
        dv = LANES
        pick_scratch = []
    return pl.pallas_call(
        functools.partial(_attn_kernel, mode=mode, lambda_init=lambda_init),
        grid=(b, n_groups),
        in_specs=in_specs,
        out_specs=pl.BlockSpec((1, s, LANES), lambda i, h: (i, 0, h)),
        out_shape=jax.ShapeDtypeStruct((b, s, n_groups * LANES), BF16),
        scratch_shapes=[
            *pick_scratch,
            pltpu.VMEM((t, t), F32),
            pltpu.VMEM((t, t), F32),
            pltpu.VMEM((t, t), F32),
            pltpu.VMEM((t, t), F32),
            pltpu.VMEM((2, 2, 1, t), F32),
            pltpu.VMEM((nt, 2, 1, t), F32),
            pltpu.VMEM((nt, 2, dv + ONES_ROWS, t), F32),
        ],
        compiler_params=_params("parallel", "parallel"),
        name=mode + "_attn",
    )(tab, qt, k, vt, *extra)


FF_CHUNK = 1024


def _post_kernel(*refs, n_mix, final_norm, next_proj):
    h_ref = refs[0]
    ys = refs[1:1 + n_mix]
    ws = refs[1 + n_mix:1 + 2 * n_mix]
    rest = refs[1 + 2 * n_mix:]
    gx_ref, wq_ref, k_ref, v_ref, wo_ref, gm_ref, wu_ref, wd_ref, gf_ref = rest[:9]
    if next_proj:
        gn_ref, wn_ref, o_ref, nqt_ref, nk_ref, nvt_ref = rest[9:]
    else:
        (o_ref,) = rest[9:]
    d = h_ref.shape[2]
    dh = d // XATTN_HEADS

    h = h_ref[0]
    for y_ref, w_ref in zip(ys, ws):
        h = h + _dot(y_ref[0], w_ref[...])

    q = _dot(_rms(h, gx_ref[...]).astype(BF16), wq_ref[...]).astype(BF16)
    outs = []
    for e in range(XATTN_HEADS):
        sl = slice(e * dh, (e + 1) * dh)
        s = _dot_nt(q[:, sl], k_ref[0, :, sl]) * (dh ** -0.5)
        p = jnp.exp(s - jnp.max(s, axis=1, keepdims=True))
        p = p / jnp.sum(p, axis=1, keepdims=True)
        outs.append(_dot(p.astype(BF16), v_ref[0, :, sl]).astype(BF16))
    h = h + _dot(jnp.concatenate(outs, axis=1), wo_ref[...])

    xn = _rms(h, gm_ref[...]).astype(BF16)
    for c in range(wu_ref.shape[1] // FF_CHUNK):
        sl = slice(c * FF_CHUNK, (c + 1) * FF_CHUNK)
        u = jnp.maximum(_dot(xn, wu_ref[:, sl]), 0.0)
        h = h + _dot((u * u).astype(BF16), wd_ref[sl, :])
    if final_norm:
        h = _rms(h, gf_ref[...])
    o_ref[0] = h
    if next_proj:
        _odd_proj_rows(h, gn_ref, wn_ref, nqt_ref, nk_ref, nvt_ref)


def _layer_spec(shape, layer, block=0):
    return pl.BlockSpec((None,) + tuple(shape), lambda *_: (layer, block, 0))


def _post(h, ys, w_mix, mix_layer, gx, wq, kv, wo, gm, wu, wd, layer, gf, final_norm, next_proj=None):
    b, s, d = h.shape
    tm = ROW_TILE
    n_mix = len(ys)
    widths = [y.shape[2] for y in ys]
    assert all(wd_ == widths[0] for wd_ in widths) and sum(widths) == w_mix.shape[1]
    tile = lambda width: pl.BlockSpec((1, tm, width), lambda i, j: (i, j, 0))
    in_specs = [tile(d)] + [tile(wd_) for wd_ in widths]
    in_specs += [_layer_spec((widths[0], d), mix_layer, block=k) for k in range(n_mix)]
    in_specs += [
        _const_spec((1, d)),
        _layer_spec(wq.shape[1:], layer),
        pl.BlockSpec((None, 1, N_MEM, d), lambda i, j: (layer, i, 0, 0)),
        pl.BlockSpec((None, 1, N_MEM, d), lambda i, j: (layer, i, 0, 1)),
        _layer_spec(wo.shape[1:], layer),
        _const_spec((1, d)),
        _layer_spec(wu.shape[1:], layer),
        _layer_spec(wd.shape[1:], layer),
        _const_spec((1, d)),
    ]
    out_specs, out_shape, extra_in = [tile(d)], [jax.ShapeDtypeStruct((b, s, d), F32)], []
    if next_proj:
        in_specs += [_const_spec((1, d)), _const_spec(next_proj[1].shape)]
        extra_in = list(next_proj)
        qkv_specs, qkv_shapes = _qkv_specs(b, s, tm, DIFF_HEADS)
        out_specs += qkv_specs
        out_shape += qkv_shapes
    outs = pl.pallas_call(
        functools.partial(_post_kernel, n_mix=n_mix, final_norm=final_norm, next_proj=bool(next_proj)),
        grid=(b, s // tm),
        in_specs=in_specs,
        out_specs=out_specs,
        out_shape=out_shape,
        compiler_params=_params("parallel", "parallel"),
        name="post",
    )(h, *ys, *([w_mix] * n_mix), gx, wq, kv, kv, wo, gm, wu, wd, gf, *extra_in)
    return outs if next_proj else outs[0]


def _memory_kv(mem2d, g, wkv):
    m, d = mem2d.shape
    depth, _, n = wkv.shape
    tm = ROW_TILE
    return pl.pallas_call(
        _norm_proj_kernel,
        grid=(depth, m // tm),
        in_specs=[
            pl.BlockSpec((tm, d), lambda l, i: (i, 0)),
            _const_spec((1, d)),
            pl.BlockSpec((None, d, n), lambda l, i: (l, 0, 0)),
        ],
        out_specs=pl.BlockSpec((None, tm, n), lambda l, i: (l, i, 0)),
        out_shape=jax.ShapeDtypeStruct((depth, m, n), BF16),
        compiler_params=_params("parallel", "parallel"),
        name="memory_kv",
    )(mem2d, g, wkv)


def _block_diag(w):
    g, n, _ = w.shape
    eye = jnp.eye(g, dtype=w.dtype)
    return (eye[:, None, :, None] * w[:, :, None, :]).reshape(g * n, g * n)


def kernel(x, mem, g_mem, g_final, mix_norm_g, xattn_norm_g, mlp_norm_g, w_in_even, conv_w, conv_b, w_rgate, b_rgate, w_igate, b_igate, lru_lambda, fox_forget_b, w_out_even, w_in_odd, lambda_q1, lambda_k1, lambda_q2, lambda_k2, diff_norm_g, w_out_odd, xattn_wq, xattn_wkv, xattn_wo, w_up, w_down):
    b, s, d = x.shape
    depth = mix_norm_g.shape[0]
    row = lambda v: v.reshape(1, -1).astype(F32)

    wq_all, wo_all = xattn_wq.astype(BF16), xattn_wo.astype(BF16)
    wu_all, wd_all = w_up.astype(BF16), w_down.astype(BF16)
    w_out_even_all, w_out_odd_all = w_out_even.astype(BF16), w_out_odd.astype(BF16)
    kv_all = _memory_kv(mem.reshape(b * N_MEM, d), row(g_mem), xattn_wkv.astype(BF16))
    kv_all = kv_all.reshape(depth, b, N_MEM, 2 * d)

    h = x
    for layer in range(depth):
        i = layer // 2
        if layer % 2 == 0:
            w_in = jnp.pad(w_in_even[i], ((0, 0), (0, 2688 - w_in_even.shape[2]))).astype(BF16)
            zl, qt, k, vt, kx = _even_proj(h, row(mix_norm_g[layer]), w_in,
                                           fox_forget_b[i].reshape(FOX_HEADS, 1).astype(F32))
            wg = jnp.concatenate([_block_diag(w_rgate[i]), _block_diag(w_igate[i])], axis=1).astype(BF16)
            bg = jnp.concatenate([b_rgate[i], b_igate[i]]).reshape(1, -1).astype(F32)
            y_lru = _lru(zl, conv_w[i].astype(F32), row(conv_b[i]), wg, bg, row(lru_lambda[i]))
            y_fox = _attention("fox", qt, k, vt, (kx,))
            ys, w_mix = [y_lru, y_fox], w_out_even_all
        else:
            qt, k, vt = odd_inputs
            lambda_init = 0.8 - 0.6 * math.exp(-0.3 * layer)
            lamv = jnp.stack([lambda_q1[i], lambda_k1[i], lambda_q2[i], lambda_k2[i]]).astype(F32)
            y = _attention("diff", qt, k, vt, (lamv, row(diff_norm_g[i])), lambda_init=lambda_init)
            ys, w_mix = [y], w_out_odd_all

        next_proj = None
        if layer % 2 == 0 and layer + 1 < depth:
            next_proj = (row(mix_norm_g[layer + 1]), w_in_odd[(layer + 1) // 2].astype(BF16))
        res = _post(h, ys, w_mix, i, row(xattn_norm_g[layer]), wq_all, kv_all, wo_all,
                    row(mlp_norm_g[layer]), wu_all, wd_all, layer,
                    row(g_final), final_norm=(layer == depth - 1), next_proj=next_proj)
        if next_proj:
            h, *odd_inputs = res
        else:
            h = res
    return h
```

```python
import functools
import math

import jax
import jax.numpy as jnp
from jax import lax
from jax.experimental import pallas as pl
from jax.experimental.pallas import tpu as pltpu

F32 = jnp.float32
BF16 = jnp.bfloat16

LANES = 128
SUBLANES = 8
VMEM_LIMIT = 56 * 1024 * 1024

EPS = 1e-6
N_MEM = 256
LRU_WIDTH = 512
LRU_BLOCKS = 8
CONV_WIDTH = 4
LRU_C = 8.0
HEAD_DIM = 64
FOX_HEADS = 8
DIFF_HEADS = 8
XATTN_HEADS = 4

LOG2E = 1.4426950408889634
QK_SCALE = HEAD_DIM ** -0.5 * LOG2E
ONES_ROWS = 16

ROW_TILE = 512
PROJ_TILE = 1024
ATTN_TILE = 512
ITEMS_PER_TRIP = 14
LRU_TILE = 1024


def _params(*sem):
    return pltpu.CompilerParams(dimension_semantics=sem, vmem_limit_bytes=VMEM_LIMIT)


def _rms(x, g):
    return x * lax.rsqrt(jnp.mean(x * x, axis=-1, keepdims=True) + EPS) * g


def _dot(a, b):
    return jnp.dot(a, b, preferred_element_type=F32)


def _dot_nt(a, b):
    return lax.dot_general(a, b, (((1,), (1,)), ((), ())), preferred_element_type=F32)


def _log_sigmoid(x):
    return jnp.minimum(x, 0.0) - jnp.log1p(jnp.exp(-jnp.abs(x)))


def _sigmoid(x):
    return 0.5 * jnp.tanh(0.5 * x) + 0.5


def _expm1_given_exp(x, u):
    near = jnp.where(u == 1.0, x, (u - 1.0) * x / jnp.log(u))
    return jnp.where(jnp.abs(x) < 0.5, near, u - 1.0)


def _const_spec(shape):
    nd = len(shape)
    return pl.BlockSpec(shape, lambda *_: (0,) * nd)


def _even_proj_kernel(x_ref, g_ref, w_ref, fb_ref, zl_ref, qt_ref, k_ref, vt_ref, kx_ref, carry_ref):
    si = pl.program_id(1)
    tm = x_ref.shape[1]
    xn = _rms(x_ref[0], g_ref[...]).astype(BF16)
    fl = _dot(xn, w_ref[:, 2560:2688])
    lf = _log_sigmoid(fl.T[0:FOX_HEADS, :] + fb_ref[...])

    @pl.when(si == 0)
    def _():
        carry_ref[...] = jnp.zeros_like(carry_ref)

    pos = lax.broadcasted_iota(jnp.int32, lf.shape, 1)
    shift = 1
    while shift < tm:
        lf = lf + jnp.where(pos >= shift, pltpu.roll(lf, shift, 1), 0.0)
        shift *= 2
    cum = lf + carry_ref[:, 0:1]
    carry_ref[...] = jnp.broadcast_to(cum[:, tm - 1:tm], carry_ref.shape)

    c = cum * (-LOG2E)
    hi = c.astype(BF16).astype(F32)
    mid = (c - hi).astype(BF16).astype(F32)
    lo = (c - hi - mid).astype(BF16).astype(F32)
    pieces = jnp.concatenate([hi, mid, lo, jnp.zeros((LANES - 3 * FOX_HEADS, tm), F32)], axis=0)
    kx_ref[0] = pieces.T.astype(BF16)

    zl_ref[0] = _dot(xn, w_ref[:, 0:1024])
    _store_group_transposed(qt_ref, _dot(xn, w_ref[:, 1024:1536]) * QK_SCALE)
    _store_groups(k_ref, _dot(xn, w_ref[:, 1536:2048]))
    _store_group_transposed(vt_ref, _dot(xn, w_ref[:, 2048:2560]))


def _store_groups(ref, z):
    for grp in range(z.shape[1] // LANES):
        ref[0, grp] = z[:, grp * LANES:(grp + 1) * LANES].astype(ref.dtype)


def _store_group_transposed(ref, z):
    zt = z.T
    for grp in range(z.shape[1] // LANES):
        ref[0, grp] = zt[grp * LANES:(grp + 1) * LANES].astype(ref.dtype)


def _qkv_specs(b, s, tm, n_groups):
    specs = [
        pl.BlockSpec((1, n_groups, LANES, tm), lambda i, j: (i, 0, 0, j)),
        pl.BlockSpec((1, n_groups, tm, LANES), lambda i, j: (i, 0, j, 0)),
        pl.BlockSpec((1, n_groups, LANES, tm), lambda i, j: (i, 0, 0, j)),
    ]
    shapes = [
        jax.ShapeDtypeStruct((b, n_groups, LANES, s), BF16),
        jax.ShapeDtypeStruct((b, n_groups, s, LANES), BF16),
        jax.ShapeDtypeStruct((b, n_groups, LANES, s), BF16),
    ]
    return specs, shapes


def _even_proj(h, g, w, fb):
    b, s, d = h.shape
    tm = PROJ_TILE
    n = w.shape[1]
    qkv_specs, qkv_shapes = _qkv_specs(b, s, tm, FOX_HEADS // 2)
    return pl.pallas_call(
        _even_proj_kernel,
        grid=(b, s // tm),
        in_specs=[
            pl.BlockSpec((1, tm, d), lambda i, j: (i, j, 0)),
            _const_spec((1, d)),
            _const_spec((d, n)),
            _const_spec((FOX_HEADS, 1)),
        ],
        out_specs=[pl.BlockSpec((1, tm, 1024), lambda i, j: (i, j, 0)), *qkv_specs,
                   pl.BlockSpec((1, tm, LANES), lambda i, j: (i, j, 0))],
        out_shape=[jax.ShapeDtypeStruct((b, s, 1024), F32), *qkv_shapes,
                   jax.ShapeDtypeStruct((b, s, LANES), BF16)],
        scratch_shapes=[pltpu.VMEM((FOX_HEADS, LANES), F32)],
        compiler_params=_params("parallel", "arbitrary"),
        name="even_proj",
    )(h, g, w, fb)


def _odd_proj_rows(h, g_ref, w_ref, qt_ref, k_ref, vt_ref):
    n = w_ref.shape[1] // 3
    xn = _rms(h, g_ref[...]).astype(BF16)
    _store_group_transposed(qt_ref, _dot(xn, w_ref[:, 0:n]) * QK_SCALE)
    _store_groups(k_ref, _dot(xn, w_ref[:, n:2 * n]))
    _store_group_transposed(vt_ref, _dot(xn, w_ref[:, 2 * n:3 * n]))


def _norm_proj_kernel(x_ref, g_ref, w_ref, o_ref):
    xn = _rms(x_ref[...], g_ref[...]).astype(BF16)
    o_ref[...] = _dot(xn, w_ref[...]).astype(o_ref.dtype)


def _gelu_tanh(x):
    return 0.5 * x * (1.0 + jnp.tanh(math.sqrt(2.0 / math.pi) * (x + 0.044715 * (x * x * x))))


def _lru_kernel(xb_ref, gb_ref, cw_ref, cb_ref, wg_ref, bg_ref, lam_ref, y_ref,
                xbuf, a_s, u_s, hc):
    ti = pl.program_id(1)
    ts = xb_ref.shape[1]
    w = LRU_WIDTH

    @pl.when(ti == 0)
    def _():
        xbuf[0:SUBLANES, :] = jnp.zeros((SUBLANES, w), F32)
        hc[...] = jnp.zeros_like(hc)

    xbuf[SUBLANES:SUBLANES + ts, :] = xb_ref[0]
    xfull = xbuf[...]
    xc = cb_ref[...]
    for k in range(CONV_WIDTH):
        lag = CONV_WIDTH - 1 - k
        shifted = pltpu.roll(xfull, lag, 0) if lag else xfull
        xc = xc + shifted[SUBLANES:SUBLANES + ts] * cw_ref[k:k + 1, :]
    xbuf[0:SUBLANES, :] = xbuf[ts:ts + SUBLANES, :]

    gates = _dot(xc.astype(BF16), wg_ref[...]) + bg_ref[...]
    r = _sigmoid(gates[:, 0:w])
    i = _sigmoid(gates[:, w:2 * w])
    log_a = LRU_C * r * _log_sigmoid(lam_ref[...])
    a = jnp.exp(log_a)
    a_s[...] = a
    v = -_expm1_given_exp(2.0 * log_a, a * a)
    sqrt_v = jnp.where(v > 0.0, v * lax.rsqrt(v), 0.0)
    u_s[...] = sqrt_v * (i * xc)

    row = lax.broadcasted_iota(jnp.int32, (SUBLANES, w), 0)

    def group(gidx, carry):
        off = pl.multiple_of(gidx * SUBLANES, SUBLANES)
        a = a_s[pl.ds(off, SUBLANES), :]
        u = u_s[pl.ds(off, SUBLANES), :]
        for sh in (1, 2, 4):
            keep = row >= sh
            u = u + a * jnp.where(keep, pltpu.roll(u, sh, 0), 0.0)
            a = a * jnp.where(keep, pltpu.roll(a, sh, 0), 1.0)
        hgrp = u + a * carry
        u_s[pl.ds(off, SUBLANES), :] = hgrp
        return jnp.broadcast_to(hgrp[SUBLANES - 1:SUBLANES, :], (SUBLANES, w))

    hc[...] = lax.fori_loop(0, ts // SUBLANES, group, hc[...], unroll=8)
    y_ref[0] = (u_s[...] * _gelu_tanh(gb_ref[0])).astype(y_ref.dtype)


def _lru(zl, cw, cb, wg, bg, lam):
    b, s, _ = zl.shape
    ts = LRU_TILE
    w = LRU_WIDTH
    return pl.pallas_call(
        _lru_kernel,
        grid=(b, s // ts),
        in_specs=[
            pl.BlockSpec((1, ts, w), lambda i, j: (i, j, 0)),
            pl.BlockSpec((1, ts, w), lambda i, j: (i, j, 1)),
            _const_spec((CONV_WIDTH, w)),
            _const_spec((1, w)),
            _const_spec((w, 2 * w)),
            _const_spec((1, 2 * w)),
            _const_spec((1, w)),
        ],
        out_specs=pl.BlockSpec((1, ts, w), lambda i, j: (i, j, 0)),
        out_shape=jax.ShapeDtypeStruct((b, s, w), BF16),
        scratch_shapes=[
            pltpu.VMEM((SUBLANES + ts, w), F32),
            pltpu.VMEM((ts, w), F32),
            pltpu.VMEM((ts, w), F32),
            pltpu.VMEM((SUBLANES, w), F32),
        ],
        compiler_params=_params("parallel", "arbitrary"),
        name="rglru",
    )(zl, zl, cw, cb, wg, bg, lam)


def _attn_items(nt):
    items = [(qi, j) for qi in range(1, nt) for j in range(qi)]
    return items + [items[-1]] * 2


def _attn_kernel(*refs, mode, lambda_init):
    if mode == "fox":
        tab_ref, qt_ref, k_ref, vt_ref, kx_ref, o_ref, pk_s, sa0, sa1, sb0, sb1, mc_s, m_s, acc_s = refs
    else:
        tab_ref, qt_ref, k_ref, vt_ref, lamv_ref, g_ref, o_ref, sa0, sa1, sb0, sb1, mc_s, m_s, acc_s = refs
    bufs = ((sa0, sa1), (sb0, sb1))
    hp = pl.program_id(1)
    t = ATTN_TILE
    s_len = k_ref.shape[2]
    nt = s_len // t
    dv = HEAD_DIM if mode == "fox" else LANES

    if mode == "fox":
        rowi = lax.broadcasted_iota(jnp.int32, (LANES, t), 0)
        for e in range(2):
            head = 2 * hp + e
            pick = (rowi == head) | (rowi == head + FOX_HEADS) | (rowi == head + 2 * FOX_HEADS)
            pk_s[e] = jnp.where(pick, 1.0, 0.0).astype(BF16)

    def q_weights(qi, e):
        qt = qt_ref[0, 0, :, pl.ds(pl.multiple_of(qi * t, t), t)]
        zero = jnp.zeros((HEAD_DIM, t), BF16)
        parts = [qt[0:HEAD_DIM], zero] if e == 0 else [zero, qt[HEAD_DIM:LANES]]
        if mode == "fox":
            parts.append(pk_s[e])
        return jnp.concatenate(parts, axis=0)

    def v_rows(e, off):
        rows = slice(e * dv, (e + 1) * dv) if mode == "fox" else slice(0, dv)
        return jnp.concatenate([vt_ref[0, 0, rows, pl.ds(off, t)], jnp.ones((ONES_ROWS, t), BF16)], axis=0)

    def colmax(x):
        acc = x[0:16]
        for i in range(1, x.shape[0] // 16):
            acc = jnp.maximum(acc, x[i * 16:(i + 1) * 16])
        return jnp.max(acc, axis=0, keepdims=True)

    def key_block(off):
        kb = k_ref[0, 0, pl.ds(off, t), :]
        if mode == "fox":
            kb = jnp.concatenate([kb, kx_ref[0, pl.ds(off, t), :]], axis=1)
        return kb

    def stage1(qi, j, par, e):
        s = _dot(key_block(pl.multiple_of(j * t, t)), q_weights(qi, e))
        bufs[par][e][...] = s
        mc_s[par, e] = colmax(s)

    def stage2(qi, j, par, e):
        off = pl.multiple_of(j * t, t)
        vt = v_rows(e, off)
        m_prev = m_s[qi, e]
        m_new = jnp.maximum(m_prev, mc_s[par, e])
        alpha = jnp.exp2(m_prev - m_new)
        p = jnp.exp2(bufs[par][e][...] - m_new).astype(BF16)
        acc_s[qi, e] = alpha * acc_s[qi, e] + _dot(vt, p)
        m_s[qi, e] = m_new

    half = t // 2

    def stage1_diag(qi, par, e):
        kb = key_block(qi * t)
        w = q_weights(qi, e)
        keys = lax.broadcasted_iota(jnp.int32, (half, half), 0)
        qpos = lax.broadcasted_iota(jnp.int32, (half, half), 1)
        tri = keys <= qpos
        s_left = jnp.where(tri, _dot(kb[0:half], w[:, 0:half]), -jnp.inf)
        s_right = _dot(kb, w[:, half:t])
        s_top = s_right[0:half]
        s_bot = jnp.where(tri, s_right[half:t], -jnp.inf)
        buf = bufs[par][e]
        buf[0:half, 0:half] = s_left
        buf[0:half, half:t] = s_top
        buf[half:t, half:t] = s_bot
        mc_s[par, e] = jnp.concatenate(
            [colmax(s_left), jnp.maximum(colmax(s_top), colmax(s_bot))], axis=1)

    def stage2_diag(qi, par, e):
        buf = bufs[par][e]
        vt = v_rows(e, qi * t)
        m_prev = m_s[qi, e]
        m_new = jnp.maximum(m_prev, mc_s[par, e])
        alpha = jnp.exp2(m_prev - m_new)
        p_left = jnp.exp2(buf[0:half, 0:half] - m_new[:, 0:half]).astype(BF16)
        p_right = jnp.exp2(buf[:, half:t] - m_new[:, half:t]).astype(BF16)
        upd = jnp.concatenate([_dot(vt[:, 0:half], p_left), _dot(vt, p_right)], axis=1)
        acc_s[qi, e] = alpha * acc_s[qi, e] + upd
        m_s[qi, e] = m_new

    def item(w):
        return tab_ref[w, 0], tab_ref[w, 1]

    m_s[...] = jnp.full(m_s.shape, -jnp.inf, F32)
    acc_s[...] = jnp.zeros(acc_s.shape, F32)

    qa, ja = item(0)
    stage1(qa, ja, 0, 0)
    stage1(qa, ja, 0, 1)

    def body(i, c):
        w = ITEMS_PER_TRIP * i
        for k in range(ITEMS_PER_TRIP):
            qc, jc = item(w + k)
            qn, jn = item(w + k + 1)
            for e in range(2):
                stage1(qn, jn, (k + 1) % 2, e)
                stage2(qc, jc, k % 2, e)
        return c

    lax.fori_loop(0, (nt * (nt - 1)) // (2 * ITEMS_PER_TRIP), body, 0)

    stage1_diag(0, 0, 0)
    stage1_diag(0, 0, 1)
    for qi in range(nt):
        par = qi % 2
        for e in range(2):
            if qi + 1 < nt:
                stage1_diag(qi + 1, 1 - par, e)
            stage2_diag(qi, par, e)

    for qi in range(nt):
        o0 = acc_s[qi, 0, 0:dv] / acc_s[qi, 0, dv:dv + 1]
        o1 = acc_s[qi, 1, 0:dv] / acc_s[qi, 1, dv:dv + 1]
        if mode == "fox":
            o = jnp.concatenate([o0, o1], axis=0).T
        else:
            lv = lamv_ref[...]
            lam = (jnp.exp(jnp.sum(lv[0:1] * lv[1:2], axis=1, keepdims=True))
                   - jnp.exp(jnp.sum(lv[2:3] * lv[3:4], axis=1, keepdims=True)) + lambda_init)
            o = (o0 - lam * o1).T
            o = o * lax.rsqrt(jnp.mean(o * o, axis=-1, keepdims=True) + EPS)
            o = o * g_ref[...] * (1.0 - lambda_init)
        o_ref[0, qi * t:(qi + 1) * t, :] = o.astype(o_ref.dtype)


def _attention(mode, qt, k, vt, extra, lambda_init=0.0):
    b, n_groups, _, s = qt.shape
    t = ATTN_TILE
    nt = s // t
    assert (nt * (nt - 1)) % (2 * ITEMS_PER_TRIP) == 0
    tab = jnp.asarray(_attn_items(nt), jnp.int32)
    in_specs = [
        pl.BlockSpec(memory_space=pltpu.SMEM),
        pl.BlockSpec((1, 1, LANES, s), lambda i, h: (i, h, 0, 0)),
        pl.BlockSpec((1, 1, s, LANES), lambda i, h: (i, h, 0, 0)),
        pl.BlockSpec((1, 1, LANES, s), lambda i, h: (i, h, 0, 0)),
    ]
    if mode == "fox":
        (kx,) = extra
        in_specs.append(pl.BlockSpec((1, s, LANES), lambda i, h: (i, 0, 0)))
        dv = HEAD_DIM
        pick_scratch = [pltpu.VMEM((2, LANES, t), BF16)]
    else:
        lamv, g = extra
        in_specs += [_const_spec(lamv.shape), _const_spec(g.shape)]
        dv = LANES
        pick_scratch = []
    return pl.pallas_call(
        functools.partial(_attn_kernel, mode=mode, lambda_init=lambda_init),
        grid=(b, n_groups),
        in_specs=in_specs,
        out_specs=pl.BlockSpec((1, s, LANES), lambda i, h: (i, 0, h)),
        out_shape=jax.ShapeDtypeStruct((b, s, n_groups * LANES), BF16),
        scratch_shapes=[
            *pick_scratch,
            pltpu.VMEM((t, t), F32),
            pltpu.VMEM((t, t), F32),
            pltpu.VMEM((t, t), F32),
            pltpu.VMEM((t, t), F32),
            pltpu.VMEM((2, 2, 1, t), F32),
            pltpu.VMEM((nt, 2, 1, t), F32),
            pltpu.VMEM((nt, 2, dv + ONES_ROWS, t), F32),
        ],
        compiler_params=_params("parallel", "parallel"),
        name=mode + "_attn",
    )(tab, qt, k, vt, *extra)


FF_CHUNK = 1024


def _post_kernel(*refs, n_mix, final_norm, next_proj):
    h_ref = refs[0]
    ys = refs[1:1 + n_mix]
    ws = refs[1 + n_mix:1 + 2 * n_mix]
    rest = refs[1 + 2 * n_mix:]
    gx_ref, qk_ref, vo_ref, gm_ref, wu_ref, wd_ref, gf_ref = rest[:7]
    if next_proj:
        gn_ref, wn_ref, o_ref, nqt_ref, nk_ref, nvt_ref = rest[7:]
    else:
        (o_ref,) = rest[7:]

    h = h_ref[0]
    for y_ref, w_ref in zip(ys, ws):
        h = h + _dot(y_ref[0], w_ref[...])

    logits = _dot(_rms(h, gx_ref[...]).astype(BF16), qk_ref[...])
    probs = []
    for e in range(XATTN_HEADS):
        s = logits[:, e * N_MEM:(e + 1) * N_MEM]
        p = jnp.exp(s - jnp.max(s, axis=1, keepdims=True))
        probs.append((p / jnp.sum(p, axis=1, keepdims=True)).astype(BF16))
    h = h + _dot(jnp.concatenate(probs, axis=1), vo_ref[...])

    xn = _rms(h, gm_ref[...]).astype(BF16)
    for c in range(wu_ref.shape[1] // FF_CHUNK):
        sl = slice(c * FF_CHUNK, (c + 1) * FF_CHUNK)
        u = jnp.maximum(_dot(xn, wu_ref[:, sl]), 0.0)
        h = h + _dot((u * u).astype(BF16), wd_ref[sl, :])
    if final_norm:
        h = _rms(h, gf_ref[...])
    o_ref[0] = h
    if next_proj:
        _odd_proj_rows(h, gn_ref, wn_ref, nqt_ref, nk_ref, nvt_ref)


def _layer_spec(shape, layer, block=0):
    return pl.BlockSpec((None,) + tuple(shape), lambda *_: (layer, block, 0))


def _post(h, ys, w_mix, mix_layer, gx, qk, vo, gm, wu, wd, layer, gf, final_norm, next_proj=None):
    b, s, d = h.shape
    tm = ROW_TILE
    n_mix = len(ys)
    widths = [y.shape[2] for y in ys]
    assert all(wd_ == widths[0] for wd_ in widths) and sum(widths) == w_mix.shape[1]
    tile = lambda width: pl.BlockSpec((1, tm, width), lambda i, j: (i, j, 0))
    in_specs = [tile(d)] + [tile(wd_) for wd_ in widths]
    in_specs += [_layer_spec((widths[0], d), mix_layer, block=k) for k in range(n_mix)]
    in_specs += [
        _const_spec((1, d)),
        pl.BlockSpec((None, None, d, d), lambda i, j: (layer, i, 0, 0)),
        pl.BlockSpec((None, None, d, d), lambda i, j: (layer, i, 0, 0)),
        _const_spec((1, d)),
        _layer_spec(wu.shape[1:], layer),
        _layer_spec(wd.shape[1:], layer),
        _const_spec((1, d)),
    ]
    out_specs, out_shape, extra_in = [tile(d)], [jax.ShapeDtypeStruct((b, s, d), F32)], []
    if next_proj:
        in_specs += [_const_spec((1, d)), _const_spec(next_proj[1].shape)]
        extra_in = list(next_proj)
        qkv_specs, qkv_shapes = _qkv_specs(b, s, tm, DIFF_HEADS)
        out_specs += qkv_specs
        out_shape += qkv_shapes
    outs = pl.pallas_call(
        functools.partial(_post_kernel, n_mix=n_mix, final_norm=final_norm, next_proj=bool(next_proj)),
        grid=(b, s // tm),
        in_specs=in_specs,
        out_specs=out_specs,
        out_shape=out_shape,
        compiler_params=_params("parallel", "parallel"),
        name="post",
    )(h, *ys, *([w_mix] * n_mix), gx, qk, vo, gm, wu, wd, gf, *extra_in)
    return outs if next_proj else outs[0]


def _fold_xattn_kernel(kv_ref, wq_ref, wo_ref, qk_ref, vo_ref):
    d = wq_ref.shape[0]
    dh = d // XATTN_HEADS
    for e in range(XATTN_HEADS):
        sl = slice(e * dh, (e + 1) * dh)
        qk_ref[:, e * N_MEM:(e + 1) * N_MEM] = (_dot_nt(wq_ref[:, sl], kv_ref[:, sl]) * dh ** -0.5).astype(BF16)
        vo_ref[e * N_MEM:(e + 1) * N_MEM, :] = _dot(kv_ref[:, d + e * dh:d + (e + 1) * dh], wo_ref[sl, :]).astype(BF16)


def _fold_xattn(kv, wq, wo):
    depth, b, _, d2 = kv.shape
    d = d2 // 2
    assert XATTN_HEADS * N_MEM == d
    per_batch = pl.BlockSpec((None, None, d, d), lambda l, i: (l, i, 0, 0))
    return pl.pallas_call(
        _fold_xattn_kernel,
        grid=(depth, b),
        in_specs=[
            pl.BlockSpec((None, None, N_MEM, d2), lambda l, i: (l, i, 0, 0)),
            pl.BlockSpec((None, d, d), lambda l, i: (l, 0, 0)),
            pl.BlockSpec((None, d, d), lambda l, i: (l, 0, 0)),
        ],
        out_specs=[per_batch, per_batch],
        out_shape=[jax.ShapeDtypeStruct((depth, b, d, d), BF16)] * 2,
        compiler_params=_params("parallel", "parallel"),
        name="fold_xattn",
    )(kv, wq, wo)


def _memory_kv(mem2d, g, wkv):
    m, d = mem2d.shape
    depth, _, n = wkv.shape
    tm = ROW_TILE
    return pl.pallas_call(
        _norm_proj_kernel,
        grid=(depth, m // tm),
        in_specs=[
            pl.BlockSpec((tm, d), lambda l, i: (i, 0)),
            _const_spec((1, d)),
            pl.BlockSpec((None, d, n), lambda l, i: (l, 0, 0)),
        ],
        out_specs=pl.BlockSpec((None, tm, n), lambda l, i: (l, i, 0)),
        out_shape=jax.ShapeDtypeStruct((depth, m, n), BF16),
        compiler_params=_params("parallel", "parallel"),
        name="memory_kv",
    )(mem2d, g, wkv)


def _block_diag(w):
    g, n, _ = w.shape
    eye = jnp.eye(g, dtype=w.dtype)
    return (eye[:, None, :, None] * w[:, :, None, :]).reshape(g * n, g * n)


def kernel(x, mem, g_mem, g_final, mix_norm_g, xattn_norm_g, mlp_norm_g, w_in_even, conv_w, conv_b, w_rgate, b_rgate, w_igate, b_igate, lru_lambda, fox_forget_b, w_out_even, w_in_odd, lambda_q1, lambda_k1, lambda_q2, lambda_k2, diff_norm_g, w_out_odd, xattn_wq, xattn_wkv, xattn_wo, w_up, w_down):
    b, s, d = x.shape
    depth = mix_norm_g.shape[0]
    row = lambda v: v.reshape(1, -1).astype(F32)

    wu_all, wd_all = w_up.astype(BF16), w_down.astype(BF16)
    w_out_even_all, w_out_odd_all = w_out_even.astype(BF16), w_out_odd.astype(BF16)
    kv_all = _memory_kv(mem.reshape(b * N_MEM, d), row(g_mem), xattn_wkv.astype(BF16))
    qk_all, vo_all = _fold_xattn(kv_all.reshape(depth, b, N_MEM, 2 * d),
                                 xattn_wq.astype(BF16), xattn_wo.astype(BF16))

    h = x
    for layer in range(depth):
        i = layer // 2
        if layer % 2 == 0:
            w_in = jnp.pad(w_in_even[i], ((0, 0), (0, 2688 - w_in_even.shape[2]))).astype(BF16)
            zl, qt, k, vt, kx = _even_proj(h, row(mix_norm_g[layer]), w_in,
                                           fox_forget_b[i].reshape(FOX_HEADS, 1).astype(F32))
            wg = jnp.concatenate([_block_diag(w_rgate[i]), _block_diag(w_igate[i])], axis=1).astype(BF16)
            bg = jnp.concatenate([b_rgate[i], b_igate[i]]).reshape(1, -1).astype(F32)
            y_lru = _lru(zl, conv_w[i].astype(F32), row(conv_b[i]), wg, bg, row(lru_lambda[i]))
            y_fox = _attention("fox", qt, k, vt, (kx,))
            ys, w_mix = [y_lru, y_fox], w_out_even_all
        else:
            qt, k, vt = odd_inputs
            lambda_init = 0.8 - 0.6 * math.exp(-0.3 * layer)
            lamv = jnp.stack([lambda_q1[i], lambda_k1[i], lambda_q2[i], lambda_k2[i]]).astype(F32)
            y = _attention("diff", qt, k, vt, (lamv, row(diff_norm_g[i])), lambda_init=lambda_init)
            ys, w_mix = [y], w_out_odd_all

        next_proj = None
        if layer % 2 == 0 and layer + 1 < depth:
            next_proj = (row(mix_norm_g[layer + 1]), w_in_odd[(layer + 1) // 2].astype(BF16))
        res = _post(h, ys, w_mix, i, row(xattn_norm_g[layer]), qk_all, vo_all,
                    row(mlp_norm_g[layer]), wu_all, wd_all, layer,
                    row(g_final), final_norm=(layer == depth - 1), next_proj=next_proj)
        if next_proj:
            h, *odd_inputs = res
        else:
            h = res
    return h
```

```python
import functools
import math

import jax
import jax.numpy as jnp
from jax import lax
from jax.experimental import pallas as pl
from jax.experimental.pallas import tpu as pltpu

F32 = jnp.float32
BF16 = jnp.bfloat16

LANES = 128
SUBLANES = 8
VMEM_LIMIT = 56 * 1024 * 1024

EPS = 1e-6
N_MEM = 256
LRU_WIDTH = 512
LRU_BLOCKS = 8
CONV_WIDTH = 4
LRU_C = 8.0
HEAD_DIM = 64
FOX_HEADS = 8
DIFF_HEADS = 8
XATTN_HEADS = 4

LOG2E = 1.4426950408889634
QK_SCALE = HEAD_DIM ** -0.5 * LOG2E
ONES_ROWS = 16

ROW_TILE = 512
PROJ_TILE = 1024
ATTN_TILE = 512
ITEMS_PER_TRIP = 14
LRU_TILE = 1024


def _params(*sem):
    return pltpu.CompilerParams(dimension_semantics=sem, vmem_limit_bytes=VMEM_LIMIT)


def _rms(x, g):
    return x * lax.rsqrt(jnp.mean(x * x, axis=-1, keepdims=True) + EPS) * g


def _dot(a, b):
    return jnp.dot(a, b, preferred_element_type=F32)


def _dot_nt(a, b):
    return lax.dot_general(a, b, (((1,), (1,)), ((), ())), preferred_element_type=F32)


def _log_sigmoid(x):
    return jnp.minimum(x, 0.0) - jnp.log1p(jnp.exp(-jnp.abs(x)))


def _sigmoid(x):
    return 0.5 * jnp.tanh(0.5 * x) + 0.5


def _expm1_given_exp(x, u):
    near = jnp.where(u == 1.0, x, (u - 1.0) * x / jnp.log(u))
    return jnp.where(jnp.abs(x) < 0.5, near, u - 1.0)


def _const_spec(shape):
    nd = len(shape)
    return pl.BlockSpec(shape, lambda *_: (0,) * nd)


def _even_proj_kernel(x_ref, g_ref, w_ref, fb_ref, zl_ref, qt_ref, k_ref, vt_ref, kx_ref, carry_ref):
    si = pl.program_id(1)
    tm = x_ref.shape[1]
    xn = _rms(x_ref[0], g_ref[...]).astype(BF16)
    fl = _dot(xn, w_ref[:, 2560:2688])
    lf = _log_sigmoid(fl.T[0:FOX_HEADS, :] + fb_ref[...])

    @pl.when(si == 0)
    def _():
        carry_ref[...] = jnp.zeros_like(carry_ref)

    pos = lax.broadcasted_iota(jnp.int32, lf.shape, 1)
    shift = 1
    while shift < tm:
        lf = lf + jnp.where(pos >= shift, pltpu.roll(lf, shift, 1), 0.0)
        shift *= 2
    cum = lf + carry_ref[:, 0:1]
    carry_ref[...] = jnp.broadcast_to(cum[:, tm - 1:tm], carry_ref.shape)

    c = cum * (-LOG2E)
    hi = c.astype(BF16).astype(F32)
    mid = (c - hi).astype(BF16).astype(F32)
    lo = (c - hi - mid).astype(BF16).astype(F32)
    pieces = jnp.concatenate([hi, mid, lo, jnp.zeros((LANES - 3 * FOX_HEADS, tm), F32)], axis=0)
    kx_ref[0] = pieces.T.astype(BF16)

    zl_ref[0] = _dot(xn, w_ref[:, 0:1024])
    _store_group_transposed(qt_ref, _dot(xn, w_ref[:, 1024:1536]) * QK_SCALE)
    _store_groups(k_ref, _dot(xn, w_ref[:, 1536:2048]))
    _store_group_transposed(vt_ref, _dot(xn, w_ref[:, 2048:2560]))


def _store_groups(ref, z):
    for grp in range(z.shape[1] // LANES):
        ref[0, grp] = z[:, grp * LANES:(grp + 1) * LANES].astype(ref.dtype)


def _store_group_transposed(ref, z):
    zt = z.T
    for grp in range(z.shape[1] // LANES):
        ref[0, grp] = zt[grp * LANES:(grp + 1) * LANES].astype(ref.dtype)


def _qkv_specs(b, s, tm, n_groups):
    specs = [
        pl.BlockSpec((1, n_groups, LANES, tm), lambda i, j: (i, 0, 0, j)),
        pl.BlockSpec((1, n_groups, tm, LANES), lambda i, j: (i, 0, j, 0)),
        pl.BlockSpec((1, n_groups, LANES, tm), lambda i, j: (i, 0, 0, j)),
    ]
    shapes = [
        jax.ShapeDtypeStruct((b, n_groups, LANES, s), BF16),
        jax.ShapeDtypeStruct((b, n_groups, s, LANES), BF16),
        jax.ShapeDtypeStruct((b, n_groups, LANES, s), BF16),
    ]
    return specs, shapes


def _even_proj(h, g, w, fb):
    b, s, d = h.shape
    tm = PROJ_TILE
    n = w.shape[1]
    qkv_specs, qkv_shapes = _qkv_specs(b, s, tm, FOX_HEADS // 2)
    return pl.pallas_call(
        _even_proj_kernel,
        grid=(b, s // tm),
        in_specs=[
            pl.BlockSpec((1, tm, d), lambda i, j: (i, j, 0)),
            _const_spec((1, d)),
            _const_spec((d, n)),
            _const_spec((FOX_HEADS, 1)),
        ],
        out_specs=[pl.BlockSpec((1, tm, 1024), lambda i, j: (i, j, 0)), *qkv_specs,
                   pl.BlockSpec((1, tm, LANES), lambda i, j: (i, j, 0))],
        out_shape=[jax.ShapeDtypeStruct((b, s, 1024), F32), *qkv_shapes,
                   jax.ShapeDtypeStruct((b, s, LANES), BF16)],
        scratch_shapes=[pltpu.VMEM((FOX_HEADS, LANES), F32)],
        compiler_params=_params("parallel", "arbitrary"),
        name="even_proj",
    )(h, g, w, fb)


def _odd_proj_rows(h, g_ref, w_ref, qt_ref, k_ref, vt_ref):
    n = w_ref.shape[1] // 3
    xn = _rms(h, g_ref[...]).astype(BF16)
    _store_group_transposed(qt_ref, _dot(xn, w_ref[:, 0:n]) * QK_SCALE)
    _store_groups(k_ref, _dot(xn, w_ref[:, n:2 * n]))
    _store_group_transposed(vt_ref, _dot(xn, w_ref[:, 2 * n:3 * n]))


def _gelu_tanh(x):
    return 0.5 * x * (1.0 + jnp.tanh(math.sqrt(2.0 / math.pi) * (x + 0.044715 * (x * x * x))))


def _lru_kernel(xb_ref, gb_ref, cw_ref, cb_ref, wg_ref, bg_ref, lam_ref, y_ref,
                xbuf, a_s, u_s, hc):
    ti = pl.program_id(1)
    ts = xb_ref.shape[1]
    w = LRU_WIDTH

    @pl.when(ti == 0)
    def _():
        xbuf[0:SUBLANES, :] = jnp.zeros((SUBLANES, w), F32)
        hc[...] = jnp.zeros_like(hc)

    xbuf[SUBLANES:SUBLANES + ts, :] = xb_ref[0]
    xfull = xbuf[...]
    xc = cb_ref[...]
    for k in range(CONV_WIDTH):
        lag = CONV_WIDTH - 1 - k
        shifted = pltpu.roll(xfull, lag, 0) if lag else xfull
        xc = xc + shifted[SUBLANES:SUBLANES + ts] * cw_ref[k:k + 1, :]
    xbuf[0:SUBLANES, :] = xbuf[ts:ts + SUBLANES, :]

    gates = _dot(xc.astype(BF16), wg_ref[...]) + bg_ref[...]
    r = _sigmoid(gates[:, 0:w])
    i = _sigmoid(gates[:, w:2 * w])
    log_a = LRU_C * r * _log_sigmoid(lam_ref[...])
    a = jnp.exp(log_a)
    a_s[...] = a
    v = -_expm1_given_exp(2.0 * log_a, a * a)
    sqrt_v = jnp.where(v > 0.0, v * lax.rsqrt(v), 0.0)
    u_s[...] = sqrt_v * (i * xc)

    row = lax.broadcasted_iota(jnp.int32, (SUBLANES, w), 0)

    def group(gidx, carry):
        off = pl.multiple_of(gidx * SUBLANES, SUBLANES)
        a = a_s[pl.ds(off, SUBLANES), :]
        u = u_s[pl.ds(off, SUBLANES), :]
        for sh in (1, 2, 4):
            keep = row >= sh
            u = u + a * jnp.where(keep, pltpu.roll(u, sh, 0), 0.0)
            a = a * jnp.where(keep, pltpu.roll(a, sh, 0), 1.0)
        hgrp = u + a * carry
        u_s[pl.ds(off, SUBLANES), :] = hgrp
        return jnp.broadcast_to(hgrp[SUBLANES - 1:SUBLANES, :], (SUBLANES, w))

    hc[...] = lax.fori_loop(0, ts // SUBLANES, group, hc[...], unroll=8)
    y_ref[0] = (u_s[...] * _gelu_tanh(gb_ref[0])).astype(y_ref.dtype)


def _lru(zl, cw, cb, wg, bg, lam):
    b, s, _ = zl.shape
    ts = LRU_TILE
    w = LRU_WIDTH
    return pl.pallas_call(
        _lru_kernel,
        grid=(b, s // ts),
        in_specs=[
            pl.BlockSpec((1, ts, w), lambda i, j: (i, j, 0)),
            pl.BlockSpec((1, ts, w), lambda i, j: (i, j, 1)),
            _const_spec((CONV_WIDTH, w)),
            _const_spec((1, w)),
            _const_spec((w, 2 * w)),
            _const_spec((1, 2 * w)),
            _const_spec((1, w)),
        ],
        out_specs=pl.BlockSpec((1, ts, w), lambda i, j: (i, j, 0)),
        out_shape=jax.ShapeDtypeStruct((b, s, w), BF16),
        scratch_shapes=[
            pltpu.VMEM((SUBLANES + ts, w), F32),
            pltpu.VMEM((ts, w), F32),
            pltpu.VMEM((ts, w), F32),
            pltpu.VMEM((SUBLANES, w), F32),
        ],
        compiler_params=_params("parallel", "arbitrary"),
        name="rglru",
    )(zl, zl, cw, cb, wg, bg, lam)


def _attn_items(nt):
    items = [(qi, j) for qi in range(1, nt) for j in range(qi)]
    return items + [items[-1]] * 2


def _attn_kernel(*refs, mode, lambda_init):
    if mode == "fox":
        tab_ref, qt_ref, k_ref, vt_ref, kx_ref, o_ref, pk_s, sa0, sa1, sb0, sb1, mc_s, m_s, acc_s = refs
    else:
        tab_ref, qt_ref, k_ref, vt_ref, lamv_ref, g_ref, o_ref, sa0, sa1, sb0, sb1, mc_s, m_s, acc_s = refs
    bufs = ((sa0, sa1), (sb0, sb1))
    hp = pl.program_id(1)
    t = ATTN_TILE
    s_len = k_ref.shape[2]
    nt = s_len // t
    dv = HEAD_DIM if mode == "fox" else LANES

    if mode == "fox":
        rowi = lax.broadcasted_iota(jnp.int32, (LANES, t), 0)
        for e in range(2):
            head = 2 * hp + e
            pick = (rowi == head) | (rowi == head + FOX_HEADS) | (rowi == head + 2 * FOX_HEADS)
            pk_s[e] = jnp.where(pick, 1.0, 0.0).astype(BF16)

    def q_weights(qi, e):
        qt = qt_ref[0, 0, :, pl.ds(pl.multiple_of(qi * t, t), t)]
        zero = jnp.zeros((HEAD_DIM, t), BF16)
        parts = [qt[0:HEAD_DIM], zero] if e == 0 else [zero, qt[HEAD_DIM:LANES]]
        if mode == "fox":
            parts.append(pk_s[e])
        return jnp.concatenate(parts, axis=0)

    def v_rows(e, off):
        rows = slice(e * dv, (e + 1) * dv) if mode == "fox" else slice(0, dv)
        return jnp.concatenate([vt_ref[0, 0, rows, pl.ds(off, t)], jnp.ones((ONES_ROWS, t), BF16)], axis=0)

    def colmax(x):
        acc = x[0:16]
        for i in range(1, x.shape[0] // 16):
            acc = jnp.maximum(acc, x[i * 16:(i + 1) * 16])
        return jnp.max(acc, axis=0, keepdims=True)

    def key_block(off):
        kb = k_ref[0, 0, pl.ds(off, t), :]
        if mode == "fox":
            kb = jnp.concatenate([kb, kx_ref[0, pl.ds(off, t), :]], axis=1)
        return kb

    def stage1(qi, j, par, e):
        s = _dot(key_block(pl.multiple_of(j * t, t)), q_weights(qi, e))
        bufs[par][e][...] = s
        mc_s[par, e] = colmax(s)

    def stage2(qi, j, par, e):
        off = pl.multiple_of(j * t, t)
        vt = v_rows(e, off)
        m_prev = m_s[qi, e]
        m_new = jnp.maximum(m_prev, mc_s[par, e])
        alpha = jnp.exp2(m_prev - m_new)
        p = jnp.exp2(bufs[par][e][...] - m_new).astype(BF16)
        acc_s[qi, e] = alpha * acc_s[qi, e] + _dot(vt, p)
        m_s[qi, e] = m_new

    half = t // 2

    def stage1_diag(qi, par, e):
        kb = key_block(qi * t)
        w = q_weights(qi, e)
        keys = lax.broadcasted_iota(jnp.int32, (half, half), 0)
        qpos = lax.broadcasted_iota(jnp.int32, (half, half), 1)
        tri = keys <= qpos
        s_left = jnp.where(tri, _dot(kb[0:half], w[:, 0:half]), -jnp.inf)
        s_right = _dot(kb, w[:, half:t])
        s_top = s_right[0:half]
        s_bot = jnp.where(tri, s_right[half:t], -jnp.inf)
        buf = bufs[par][e]
        buf[0:half, 0:half] = s_left
        buf[0:half, half:t] = s_top
        buf[half:t, half:t] = s_bot
        mc_s[par, e] = jnp.concatenate(
            [colmax(s_left), jnp.maximum(colmax(s_top), colmax(s_bot))], axis=1)

    def stage2_diag(qi, par, e):
        buf = bufs[par][e]
        vt = v_rows(e, qi * t)
        m_prev = m_s[qi, e]
        m_new = jnp.maximum(m_prev, mc_s[par, e])
        alpha = jnp.exp2(m_prev - m_new)
        p_left = jnp.exp2(buf[0:half, 0:half] - m_new[:, 0:half]).astype(BF16)
        p_right = jnp.exp2(buf[:, half:t] - m_new[:, half:t]).astype(BF16)
        upd = jnp.concatenate([_dot(vt[:, 0:half], p_left), _dot(vt, p_right)], axis=1)
        acc_s[qi, e] = alpha * acc_s[qi, e] + upd
        m_s[qi, e] = m_new

    def item(w):
        return tab_ref[w, 0], tab_ref[w, 1]

    m_s[...] = jnp.full(m_s.shape, -jnp.inf, F32)
    acc_s[...] = jnp.zeros(acc_s.shape, F32)

    qa, ja = item(0)
    stage1(qa, ja, 0, 0)
    stage1(qa, ja, 0, 1)

    def body(i, c):
        w = ITEMS_PER_TRIP * i
        for k in range(ITEMS_PER_TRIP):
            qc, jc = item(w + k)
            qn, jn = item(w + k + 1)
            for e in range(2):
                stage1(qn, jn, (k + 1) % 2, e)
                stage2(qc, jc, k % 2, e)
        return c

    lax.fori_loop(0, (nt * (nt - 1)) // (2 * ITEMS_PER_TRIP), body, 0)

    stage1_diag(0, 0, 0)
    stage1_diag(0, 0, 1)
    for qi in range(nt):
        par = qi % 2
        for e in range(2):
            if qi + 1 < nt:
                stage1_diag(qi + 1, 1 - par, e)
            stage2_diag(qi, par, e)

    for qi in range(nt):
        o0 = acc_s[qi, 0, 0:dv] / acc_s[qi, 0, dv:dv + 1]
        o1 = acc_s[qi, 1, 0:dv] / acc_s[qi, 1, dv:dv + 1]
        if mode == "fox":
            o = jnp.concatenate([o0, o1], axis=0).T
        else:
            lv = lamv_ref[...]
            lam = (jnp.exp(jnp.sum(lv[0:1] * lv[1:2], axis=1, keepdims=True))
                   - jnp.exp(jnp.sum(lv[2:3] * lv[3:4], axis=1, keepdims=True)) + lambda_init)
            o = (o0 - lam * o1).T
            o = o * lax.rsqrt(jnp.mean(o * o, axis=-1, keepdims=True) + EPS)
            o = o * g_ref[...] * (1.0 - lambda_init)
        o_ref[0, qi * t:(qi + 1) * t, :] = o.astype(o_ref.dtype)


def _attention(mode, qt, k, vt, extra, lambda_init=0.0):
    b, n_groups, _, s = qt.shape
    t = ATTN_TILE
    nt = s // t
    assert (nt * (nt - 1)) % (2 * ITEMS_PER_TRIP) == 0
    tab = jnp.asarray(_attn_items(nt), jnp.int32)
    in_specs = [
        pl.BlockSpec(memory_space=pltpu.SMEM),
        pl.BlockSpec((1, 1, LANES, s), lambda i, h: (i, h, 0, 0)),
        pl.BlockSpec((1, 1, s, LANES), lambda i, h: (i, h, 0, 0)),
        pl.BlockSpec((1, 1, LANES, s), lambda i, h: (i, h, 0, 0)),
    ]
    if mode == "fox":
        (kx,) = extra
        in_specs.append(pl.BlockSpec((1, s, LANES), lambda i, h: (i, 0, 0)))
        dv = HEAD_DIM
        pick_scratch = [pltpu.VMEM((2, LANES, t), BF16)]
    else:
        lamv, g = extra
        in_specs += [_const_spec(lamv.shape), _const_spec(g.shape)]
        dv = LANES
        pick_scratch = []
    return pl.pallas_call(
        functools.partial(_attn_kernel, mode=mode, lambda_init=lambda_init),
        grid=(b, n_groups),
        in_specs=in_specs,
        out_specs=pl.BlockSpec((1, s, LANES), lambda i, h: (i, 0, h)),
        out_shape=jax.ShapeDtypeStruct((b, s, n_groups * LANES), BF16),
        scratch_shapes=[
            *pick_scratch,
            pltpu.VMEM((t, t), F32),
            pltpu.VMEM((t, t), F32),
            pltpu.VMEM((t, t), F32),
            pltpu.VMEM((t, t), F32),
            pltpu.VMEM((2, 2, 1, t), F32),
            pltpu.VMEM((nt, 2, 1, t), F32),
            pltpu.VMEM((nt, 2, dv + ONES_ROWS, t), F32),
        ],
        compiler_params=_params("parallel", "parallel"),
        name=mode + "_attn",
    )(tab, qt, k, vt, *extra)


FF_CHUNK = 512


def _post_kernel(*refs, n_mix, final_norm, next_proj):
    h_ref = refs[0]
    ys = refs[1:1 + n_mix]
    ws = refs[1 + n_mix:1 + 2 * n_mix]
    rest = refs[1 + 2 * n_mix:]
    gx_ref, qk_ref, vo_ref, gm_ref, wu_ref, wd_ref, gf_ref = rest[:7]
    if next_proj:
        gn_ref, wn_ref, o_ref, nqt_ref, nk_ref, nvt_ref = rest[7:]
    else:
        (o_ref,) = rest[7:]

    h = h_ref[0]
    for y_ref, w_ref in zip(ys, ws):
        h = h + _dot(y_ref[0], w_ref[...])

    logits = _dot(_rms(h, gx_ref[...]).astype(BF16), qk_ref[...])
    probs = []
    for e in range(XATTN_HEADS):
        s = logits[:, e * N_MEM:(e + 1) * N_MEM]
        p = jnp.exp(s - jnp.max(s, axis=1, keepdims=True))
        probs.append((p / jnp.sum(p, axis=1, keepdims=True)).astype(BF16))
    h = h + _dot(jnp.concatenate(probs, axis=1), vo_ref[...])

    xn = _rms(h, gm_ref[...]).astype(BF16)
    for c in range(wu_ref.shape[1] // FF_CHUNK):
        sl = slice(c * FF_CHUNK, (c + 1) * FF_CHUNK)
        u = jnp.maximum(_dot(xn, wu_ref[:, sl]), 0.0)
        h = h + _dot((u * u).astype(BF16), wd_ref[sl, :])
    if final_norm:
        h = _rms(h, gf_ref[...])
    o_ref[0] = h
    if next_proj:
        _odd_proj_rows(h, gn_ref, wn_ref, nqt_ref, nk_ref, nvt_ref)


def _layer_spec(shape, layer, block=0):
    return pl.BlockSpec((None,) + tuple(shape), lambda *_: (layer, block, 0))


def _post(h, ys, w_mix, mix_layer, gx, qk, vo, gm, wu, wd, layer, gf, final_norm, next_proj=None):
    b, s, d = h.shape
    tm = ROW_TILE
    n_mix = len(ys)
    widths = [y.shape[2] for y in ys]
    assert all(wd_ == widths[0] for wd_ in widths) and sum(widths) == w_mix.shape[1]
    tile = lambda width: pl.BlockSpec((1, tm, width), lambda i, j: (i, j, 0))
    in_specs = [tile(d)] + [tile(wd_) for wd_ in widths]
    in_specs += [_layer_spec((widths[0], d), mix_layer, block=k) for k in range(n_mix)]
    in_specs += [
        _const_spec((1, d)),
        pl.BlockSpec((None, None, d, d), lambda i, j: (layer, i, 0, 0)),
        pl.BlockSpec((None, None, d, d), lambda i, j: (layer, i, 0, 0)),
        _const_spec((1, d)),
        _layer_spec(wu.shape[1:], layer),
        _layer_spec(wd.shape[1:], layer),
        _const_spec((1, d)),
    ]
    out_specs, out_shape, extra_in = [tile(d)], [jax.ShapeDtypeStruct((b, s, d), F32)], []
    if next_proj:
        in_specs += [_const_spec((1, d)), _const_spec(next_proj[1].shape)]
        extra_in = list(next_proj)
        qkv_specs, qkv_shapes = _qkv_specs(b, s, tm, DIFF_HEADS)
        out_specs += qkv_specs
        out_shape += qkv_shapes
    outs = pl.pallas_call(
        functools.partial(_post_kernel, n_mix=n_mix, final_norm=final_norm, next_proj=bool(next_proj)),
        grid=(b, s // tm),
        in_specs=in_specs,
        out_specs=out_specs,
        out_shape=out_shape,
        compiler_params=_params("parallel", "parallel"),
        name="post",
    )(h, *ys, *([w_mix] * n_mix), gx, qk, vo, gm, wu, wd, gf, *extra_in)
    return outs if next_proj else outs[0]


def _fold_xattn_kernel(mem_ref, g_ref, wkv_ref, wq_ref, wo_ref, qk_ref, vo_ref):
    d = wq_ref.shape[0]
    dh = d // XATTN_HEADS
    kv = _dot(_rms(mem_ref[...], g_ref[...]).astype(BF16), wkv_ref[...]).astype(BF16)
    for e in range(XATTN_HEADS):
        sl = slice(e * dh, (e + 1) * dh)
        qk_ref[:, e * N_MEM:(e + 1) * N_MEM] = (_dot_nt(wq_ref[:, sl], kv[:, sl]) * dh ** -0.5).astype(BF16)
        vo_ref[e * N_MEM:(e + 1) * N_MEM, :] = _dot(kv[:, d + e * dh:d + (e + 1) * dh], wo_ref[sl, :]).astype(BF16)


def _fold_xattn(mem, g, wkv, wq, wo):
    b, n_mem, d = mem.shape
    depth = wkv.shape[0]
    assert n_mem == N_MEM and XATTN_HEADS * N_MEM == d
    per_layer = lambda cols: pl.BlockSpec((None, d, cols), lambda l, i: (l, 0, 0))
    per_batch = pl.BlockSpec((None, None, d, d), lambda l, i: (l, i, 0, 0))
    return pl.pallas_call(
        _fold_xattn_kernel,
        grid=(depth, b),
        in_specs=[
            pl.BlockSpec((None, N_MEM, d), lambda l, i: (i, 0, 0)),
            _const_spec((1, d)),
            per_layer(2 * d),
            per_layer(d),
            per_layer(d),
        ],
        out_specs=[per_batch, per_batch],
        out_shape=[jax.ShapeDtypeStruct((depth, b, d, d), BF16)] * 2,
        compiler_params=_params("parallel", "parallel"),
        name="fold_xattn",
    )(mem, g, wkv, wq, wo)


def _block_diag(w):
    g, n, _ = w.shape
    eye = jnp.eye(g, dtype=w.dtype)
    return (eye[:, None, :, None] * w[:, :, None, :]).reshape(g * n, g * n)


def kernel(x, mem, g_mem, g_final, mix_norm_g, xattn_norm_g, mlp_norm_g, w_in_even, conv_w, conv_b, w_rgate, b_rgate, w_igate, b_igate, lru_lambda, fox_forget_b, w_out_even, w_in_odd, lambda_q1, lambda_k1, lambda_q2, lambda_k2, diff_norm_g, w_out_odd, xattn_wq, xattn_wkv, xattn_wo, w_up, w_down):
    b, s, d = x.shape
    depth = mix_norm_g.shape[0]
    row = lambda v: v.reshape(1, -1).astype(F32)

    wu_all, wd_all = w_up.astype(BF16), w_down.astype(BF16)
    w_out_even_all, w_out_odd_all = w_out_even.astype(BF16), w_out_odd.astype(BF16)
    qk_all, vo_all = _fold_xattn(mem, row(g_mem), xattn_wkv.astype(BF16),
                                 xattn_wq.astype(BF16), xattn_wo.astype(BF16))

    h = x
    for layer in range(depth):
        i = layer // 2
        if layer % 2 == 0:
            w_in = jnp.pad(w_in_even[i], ((0, 0), (0, 2688 - w_in_even.shape[2]))).astype(BF16)
            zl, qt, k, vt, kx = _even_proj(h, row(mix_norm_g[layer]), w_in,
                                           fox_forget_b[i].reshape(FOX_HEADS, 1).astype(F32))
            wg = jnp.concatenate([_block_diag(w_rgate[i]), _block_diag(w_igate[i])], axis=1).astype(BF16)
            bg = jnp.concatenate([b_rgate[i], b_igate[i]]).reshape(1, -1).astype(F32)
            y_lru = _lru(zl, conv_w[i].astype(F32), row(conv_b[i]), wg, bg, row(lru_lambda[i]))
            y_fox = _attention("fox", qt, k, vt, (kx,))
            ys, w_mix = [y_lru, y_fox], w_out_even_all
        else:
            qt, k, vt = odd_inputs
            lambda_init = 0.8 - 0.6 * math.exp(-0.3 * layer)
            lamv = jnp.stack([lambda_q1[i], lambda_k1[i], lambda_q2[i], lambda_k2[i]]).astype(F32)
            y = _attention("diff", qt, k, vt, (lamv, row(diff_norm_g[i])), lambda_init=lambda_init)
            ys, w_mix = [y], w_out_odd_all

        next_proj = None
        if layer % 2 == 0 and layer + 1 < depth:
            next_proj = (row(mix_norm_g[layer + 1]), w_in_odd[(layer + 1) // 2].astype(BF16))
        res = _post(h, ys, w_mix, i, row(xattn_norm_g[layer]), qk_all, vo_all,
                    row(mlp_norm_g[layer]), wu_all, wd_all, layer,
                    row(g_final), final_norm=(layer == depth - 1), next_proj=next_proj)
        if next_proj:
            h, *odd_inputs = res
        else:
            h = res
    return h
```

```python
import functools
import math

import jax
import jax.numpy as jnp
from jax import lax
from jax.experimental import pallas as pl
from jax.experimental.pallas import tpu as pltpu

F32 = jnp.float32
BF16 = jnp.bfloat16

LANES = 128
SUBLANES = 8
VMEM_LIMIT = 56 * 1024 * 1024

EPS = 1e-6
N_MEM = 256
LRU_WIDTH = 512
LRU_BLOCKS = 8
CONV_WIDTH = 4
LRU_C = 8.0
HEAD_DIM = 64
FOX_HEADS = 8
DIFF_HEADS = 8
XATTN_HEADS = 4

LOG2E = 1.4426950408889634
QK_SCALE = HEAD_DIM ** -0.5 * LOG2E
ONES_ROWS = 16

ROW_TILE = 512
PROJ_TILE = 1024
ATTN_TILE = 512
ITEMS_PER_TRIP = 14
LRU_TILE = 1024


def _params(*sem):
    return pltpu.CompilerParams(dimension_semantics=sem, vmem_limit_bytes=VMEM_LIMIT)


def _rms(x, g):
    return x * lax.rsqrt(jnp.mean(x * x, axis=-1, keepdims=True) + EPS) * g


def _dot(a, b):
    return jnp.dot(a, b, preferred_element_type=F32)


def _dot_nt(a, b):
    return lax.dot_general(a, b, (((1,), (1,)), ((), ())), preferred_element_type=F32)


def _log_sigmoid(x):
    return jnp.minimum(x, 0.0) - jnp.log1p(jnp.exp(-jnp.abs(x)))


def _sigmoid(x):
    return 0.5 * jnp.tanh(0.5 * x) + 0.5


def _expm1_given_exp(x, u):
    near = jnp.where(u == 1.0, x, (u - 1.0) * x / jnp.log(u))
    return jnp.where(jnp.abs(x) < 0.5, near, u - 1.0)


def _const_spec(shape):
    nd = len(shape)
    return pl.BlockSpec(shape, lambda *_: (0,) * nd)


def _even_proj_kernel(x_ref, g_ref, w_ref, fb_ref, zl_ref, qt_ref, k_ref, vt_ref, kx_ref, carry_ref):
    si = pl.program_id(1)
    tm = x_ref.shape[1]
    xn = _rms(x_ref[0], g_ref[...]).astype(BF16)
    fl = _dot(xn, w_ref[:, 2560:2688])
    lf = _log_sigmoid(fl.T[0:FOX_HEADS, :] + fb_ref[...])

    @pl.when(si == 0)
    def _():
        carry_ref[...] = jnp.zeros_like(carry_ref)

    pos = lax.broadcasted_iota(jnp.int32, lf.shape, 1)
    shift = 1
    while shift < tm:
        lf = lf + jnp.where(pos >= shift, pltpu.roll(lf, shift, 1), 0.0)
        shift *= 2
    cum = lf + carry_ref[:, 0:1]
    carry_ref[...] = jnp.broadcast_to(cum[:, tm - 1:tm], carry_ref.shape)

    c = cum * (-LOG2E)
    hi = c.astype(BF16).astype(F32)
    mid = (c - hi).astype(BF16).astype(F32)
    lo = (c - hi - mid).astype(BF16).astype(F32)
    pieces = jnp.concatenate([hi, mid, lo, jnp.zeros((LANES - 3 * FOX_HEADS, tm), F32)], axis=0)
    kx_ref[0] = pieces.T.astype(BF16)

    zl_ref[0] = _dot(xn, w_ref[:, 0:1024])
    _store_group_transposed(qt_ref, _dot(xn, w_ref[:, 1024:1536]) * QK_SCALE)
    _store_groups(k_ref, _dot(xn, w_ref[:, 1536:2048]))
    _store_group_transposed(vt_ref, _dot(xn, w_ref[:, 2048:2560]))


def _store_groups(ref, z):
    for grp in range(z.shape[1] // LANES):
        ref[0, grp] = z[:, grp * LANES:(grp + 1) * LANES].astype(ref.dtype)


def _store_group_transposed(ref, z):
    zt = z.T
    for grp in range(z.shape[1] // LANES):
        ref[0, grp] = zt[grp * LANES:(grp + 1) * LANES].astype(ref.dtype)


def _qkv_specs(b, s, tm, n_groups):
    specs = [
        pl.BlockSpec((1, n_groups, LANES, tm), lambda i, j: (i, 0, 0, j)),
        pl.BlockSpec((1, n_groups, tm, LANES), lambda i, j: (i, 0, j, 0)),
        pl.BlockSpec((1, n_groups, LANES, tm), lambda i, j: (i, 0, 0, j)),
    ]
    shapes = [
        jax.ShapeDtypeStruct((b, n_groups, LANES, s), BF16),
        jax.ShapeDtypeStruct((b, n_groups, s, LANES), BF16),
        jax.ShapeDtypeStruct((b, n_groups, LANES, s), BF16),
    ]
    return specs, shapes


def _even_proj(h, g, w, fb):
    b, s, d = h.shape
    tm = PROJ_TILE
    n = w.shape[1]
    qkv_specs, qkv_shapes = _qkv_specs(b, s, tm, FOX_HEADS // 2)
    return pl.pallas_call(
        _even_proj_kernel,
        grid=(b, s // tm),
        in_specs=[
            pl.BlockSpec((1, tm, d), lambda i, j: (i, j, 0)),
            _const_spec((1, d)),
            _const_spec((d, n)),
            _const_spec((FOX_HEADS, 1)),
        ],
        out_specs=[pl.BlockSpec((1, tm, 1024), lambda i, j: (i, j, 0)), *qkv_specs,
                   pl.BlockSpec((1, tm, LANES), lambda i, j: (i, j, 0))],
        out_shape=[jax.ShapeDtypeStruct((b, s, 1024), F32), *qkv_shapes,
                   jax.ShapeDtypeStruct((b, s, LANES), BF16)],
        scratch_shapes=[pltpu.VMEM((FOX_HEADS, LANES), F32)],
        compiler_params=_params("parallel", "arbitrary"),
        name="even_proj",
    )(h, g, w, fb)


def _odd_proj_rows(h, g_ref, w_ref, qt_ref, k_ref, vt_ref):
    n = w_ref.shape[1] // 3
    xn = _rms(h, g_ref[...]).astype(BF16)
    _store_group_transposed(qt_ref, _dot(xn, w_ref[:, 0:n]) * QK_SCALE)
    _store_groups(k_ref, _dot(xn, w_ref[:, n:2 * n]))
    _store_group_transposed(vt_ref, _dot(xn, w_ref[:, 2 * n:3 * n]))


def _gelu_tanh(x):
    return 0.5 * x * (1.0 + jnp.tanh(math.sqrt(2.0 / math.pi) * (x + 0.044715 * (x * x * x))))


def _lru_kernel(xb_ref, gb_ref, cw_ref, cb_ref, wg_ref, bg_ref, lam_ref, y_ref,
                xbuf, a_s, u_s, hc):
    ti = pl.program_id(1)
    ts = xb_ref.shape[1]
    w = LRU_WIDTH

    @pl.when(ti == 0)
    def _():
        xbuf[0:SUBLANES, :] = jnp.zeros((SUBLANES, w), F32)
        hc[...] = jnp.zeros_like(hc)

    xbuf[SUBLANES:SUBLANES + ts, :] = xb_ref[0]
    xfull = xbuf[...]
    xc = cb_ref[...]
    for k in range(CONV_WIDTH):
        lag = CONV_WIDTH - 1 - k
        shifted = pltpu.roll(xfull, lag, 0) if lag else xfull
        xc = xc + shifted[SUBLANES:SUBLANES + ts] * cw_ref[k:k + 1, :]
    xbuf[0:SUBLANES, :] = xbuf[ts:ts + SUBLANES, :]

    gates = _dot(xc.astype(BF16), wg_ref[...]) + bg_ref[...]
    r = _sigmoid(gates[:, 0:w])
    i = _sigmoid(gates[:, w:2 * w])
    log_a = LRU_C * r * _log_sigmoid(lam_ref[...])
    a = jnp.exp(log_a)
    a_s[...] = a
    v = -_expm1_given_exp(2.0 * log_a, a * a)
    sqrt_v = jnp.where(v > 0.0, v * lax.rsqrt(v), 0.0)
    u_s[...] = sqrt_v * (i * xc)

    row = lax.broadcasted_iota(jnp.int32, (SUBLANES, w), 0)

    def group(gidx, carry):
        off = pl.multiple_of(gidx * SUBLANES, SUBLANES)
        a = a_s[pl.ds(off, SUBLANES), :]
        u = u_s[pl.ds(off, SUBLANES), :]
        for sh in (1, 2, 4):
            keep = row >= sh
            u = u + a * jnp.where(keep, pltpu.roll(u, sh, 0), 0.0)
            a = a * jnp.where(keep, pltpu.roll(a, sh, 0), 1.0)
        hgrp = u + a * carry
        u_s[pl.ds(off, SUBLANES), :] = hgrp
        return jnp.broadcast_to(hgrp[SUBLANES - 1:SUBLANES, :], (SUBLANES, w))

    hc[...] = lax.fori_loop(0, ts // SUBLANES, group, hc[...], unroll=8)
    y_ref[0] = (u_s[...] * _gelu_tanh(gb_ref[0])).astype(y_ref.dtype)


def _lru(zl, cw, cb, wg, bg, lam):
    b, s, _ = zl.shape
    ts = LRU_TILE
    w = LRU_WIDTH
    return pl.pallas_call(
        _lru_kernel,
        grid=(b, s // ts),
        in_specs=[
            pl.BlockSpec((1, ts, w), lambda i, j: (i, j, 0)),
            pl.BlockSpec((1, ts, w), lambda i, j: (i, j, 1)),
            _const_spec((CONV_WIDTH, w)),
            _const_spec((1, w)),
            _const_spec((w, 2 * w)),
            _const_spec((1, 2 * w)),
            _const_spec((1, w)),
        ],
        out_specs=pl.BlockSpec((1, ts, w), lambda i, j: (i, j, 0)),
        out_shape=jax.ShapeDtypeStruct((b, s, w), BF16),
        scratch_shapes=[
            pltpu.VMEM((SUBLANES + ts, w), F32),
            pltpu.VMEM((ts, w), F32),
            pltpu.VMEM((ts, w), F32),
            pltpu.VMEM((SUBLANES, w), F32),
        ],
        compiler_params=_params("parallel", "arbitrary"),
        name="rglru",
    )(zl, zl, cw, cb, wg, bg, lam)


def _attn_items(nt):
    items = [(qi, j) for qi in range(1, nt) for j in range(qi)]
    return items + [items[-1]] * 2


def _attn_kernel(*refs, mode, lambda_init, n_side):
    n_in = 5 if mode == "fox" else 6
    side_in = refs[n_in:n_in + n_side]
    side_out = refs[n_in + n_side + 1:n_in + 2 * n_side + 1]
    refs = refs[:n_in] + refs[n_in + n_side:n_in + n_side + 1] + refs[n_in + 2 * n_side + 1:]
    if mode == "fox":
        tab_ref, qt_ref, k_ref, vt_ref, kx_ref, o_ref, pk_s, sa0, sa1, sb0, sb1, mc_s, m_s, acc_s = refs
    else:
        tab_ref, qt_ref, k_ref, vt_ref, lamv_ref, g_ref, o_ref, sa0, sa1, sb0, sb1, mc_s, m_s, acc_s = refs
    bufs = ((sa0, sa1), (sb0, sb1))
    hp = pl.program_id(1)
    t = ATTN_TILE
    s_len = k_ref.shape[2]
    nt = s_len // t
    dv = HEAD_DIM if mode == "fox" else LANES

    if mode == "fox":
        rowi = lax.broadcasted_iota(jnp.int32, (LANES, t), 0)
        for e in range(2):
            head = 2 * hp + e
            pick = (rowi == head) | (rowi == head + FOX_HEADS) | (rowi == head + 2 * FOX_HEADS)
            pk_s[e] = jnp.where(pick, 1.0, 0.0).astype(BF16)

    def q_weights(qi, e):
        qt = qt_ref[0, 0, :, pl.ds(pl.multiple_of(qi * t, t), t)]
        zero = jnp.zeros((HEAD_DIM, t), BF16)
        parts = [qt[0:HEAD_DIM], zero] if e == 0 else [zero, qt[HEAD_DIM:LANES]]
        if mode == "fox":
            parts.append(pk_s[e])
        return jnp.concatenate(parts, axis=0)

    def v_rows(e, off):
        rows = slice(e * dv, (e + 1) * dv) if mode == "fox" else slice(0, dv)
        return jnp.concatenate([vt_ref[0, 0, rows, pl.ds(off, t)], jnp.ones((ONES_ROWS, t), BF16)], axis=0)

    def colmax(x):
        acc = x[0:16]
        for i in range(1, x.shape[0] // 16):
            acc = jnp.maximum(acc, x[i * 16:(i + 1) * 16])
        return jnp.max(acc, axis=0, keepdims=True)

    def key_block(off):
        kb = k_ref[0, 0, pl.ds(off, t), :]
        if mode == "fox":
            kb = jnp.concatenate([kb, kx_ref[0, pl.ds(off, t), :]], axis=1)
        return kb

    def stage1(qi, j, par, e):
        s = _dot(key_block(pl.multiple_of(j * t, t)), q_weights(qi, e))
        bufs[par][e][...] = s
        mc_s[par, e] = colmax(s)

    def stage2(qi, j, par, e):
        off = pl.multiple_of(j * t, t)
        vt = v_rows(e, off)
        m_prev = m_s[qi, e]
        m_new = jnp.maximum(m_prev, mc_s[par, e])
        alpha = jnp.exp2(m_prev - m_new)
        p = jnp.exp2(bufs[par][e][...] - m_new).astype(BF16)
        acc_s[qi, e] = alpha * acc_s[qi, e] + _dot(vt, p)
        m_s[qi, e] = m_new

    half = t // 2

    def stage1_diag(qi, par, e):
        kb = key_block(qi * t)
        w = q_weights(qi, e)
        keys = lax.broadcasted_iota(jnp.int32, (half, half), 0)
        qpos = lax.broadcasted_iota(jnp.int32, (half, half), 1)
        tri = keys <= qpos
        s_left = jnp.where(tri, _dot(kb[0:half], w[:, 0:half]), -jnp.inf)
        s_right = _dot(kb, w[:, half:t])
        s_top = s_right[0:half]
        s_bot = jnp.where(tri, s_right[half:t], -jnp.inf)
        buf = bufs[par][e]
        buf[0:half, 0:half] = s_left
        buf[0:half, half:t] = s_top
        buf[half:t, half:t] = s_bot
        mc_s[par, e] = jnp.concatenate(
            [colmax(s_left), jnp.maximum(colmax(s_top), colmax(s_bot))], axis=1)

    def stage2_diag(qi, par, e):
        buf = bufs[par][e]
        vt = v_rows(e, qi * t)
        m_prev = m_s[qi, e]
        m_new = jnp.maximum(m_prev, mc_s[par, e])
        alpha = jnp.exp2(m_prev - m_new)
        p_left = jnp.exp2(buf[0:half, 0:half] - m_new[:, 0:half]).astype(BF16)
        p_right = jnp.exp2(buf[:, half:t] - m_new[:, half:t]).astype(BF16)
        upd = jnp.concatenate([_dot(vt[:, 0:half], p_left), _dot(vt, p_right)], axis=1)
        acc_s[qi, e] = alpha * acc_s[qi, e] + upd
        m_s[qi, e] = m_new

    def item(w):
        return tab_ref[w, 0], tab_ref[w, 1]

    m_s[...] = jnp.full(m_s.shape, -jnp.inf, F32)
    acc_s[...] = jnp.zeros(acc_s.shape, F32)

    qa, ja = item(0)
    stage1(qa, ja, 0, 0)
    stage1(qa, ja, 0, 1)

    def body(i, c):
        w = ITEMS_PER_TRIP * i
        for k in range(ITEMS_PER_TRIP):
            qc, jc = item(w + k)
            qn, jn = item(w + k + 1)
            for e in range(2):
                stage1(qn, jn, (k + 1) % 2, e)
                stage2(qc, jc, k % 2, e)
        return c

    lax.fori_loop(0, (nt * (nt - 1)) // (2 * ITEMS_PER_TRIP), body, 0)

    for src, dst in zip(side_in, side_out):
        dst[...] = src[...].astype(dst.dtype)

    stage1_diag(0, 0, 0)
    stage1_diag(0, 0, 1)
    for qi in range(nt):
        par = qi % 2
        for e in range(2):
            if qi + 1 < nt:
                stage1_diag(qi + 1, 1 - par, e)
            stage2_diag(qi, par, e)

    for qi in range(nt):
        o0 = acc_s[qi, 0, 0:dv] / acc_s[qi, 0, dv:dv + 1]
        o1 = acc_s[qi, 1, 0:dv] / acc_s[qi, 1, dv:dv + 1]
        if mode == "fox":
            o = jnp.concatenate([o0, o1], axis=0).T
        else:
            lv = lamv_ref[...]
            lam = (jnp.exp(jnp.sum(lv[0:1] * lv[1:2], axis=1, keepdims=True))
                   - jnp.exp(jnp.sum(lv[2:3] * lv[3:4], axis=1, keepdims=True)) + lambda_init)
            o = (o0 - lam * o1).T
            o = o * lax.rsqrt(jnp.mean(o * o, axis=-1, keepdims=True) + EPS)
            o = o * g_ref[...] * (1.0 - lambda_init)
        o_ref[0, qi * t:(qi + 1) * t, :] = o.astype(o_ref.dtype)


def _attention(mode, qt, k, vt, extra, lambda_init=0.0, side_casts=()):
    b, n_groups, _, s = qt.shape
    t = ATTN_TILE
    nt = s // t
    assert (nt * (nt - 1)) % (2 * ITEMS_PER_TRIP) == 0
    tab = jnp.asarray(_attn_items(nt), jnp.int32)
    in_specs = [
        pl.BlockSpec(memory_space=pltpu.SMEM),
        pl.BlockSpec((1, 1, LANES, s), lambda i, h: (i, h, 0, 0)),
        pl.BlockSpec((1, 1, s, LANES), lambda i, h: (i, h, 0, 0)),
        pl.BlockSpec((1, 1, LANES, s), lambda i, h: (i, h, 0, 0)),
    ]
    if mode == "fox":
        (kx,) = extra
        in_specs.append(pl.BlockSpec((1, s, LANES), lambda i, h: (i, 0, 0)))
        dv = HEAD_DIM
        pick_scratch = [pltpu.VMEM((2, LANES, t), BF16)]
    else:
        lamv, g = extra
        in_specs += [_const_spec(lamv.shape), _const_spec(g.shape)]
        dv = LANES
        pick_scratch = []
    side_specs = [pl.BlockSpec((1,) + w.shape[1:], lambda i, h: (i * n_groups + h, 0, 0)) for w in side_casts]
    assert all(w.shape[0] == b * n_groups for w in side_casts)
    outs = pl.pallas_call(
        functools.partial(_attn_kernel, mode=mode, lambda_init=lambda_init, n_side=len(side_casts)),
        grid=(b, n_groups),
        in_specs=in_specs + side_specs,
        out_specs=[pl.BlockSpec((1, s, LANES), lambda i, h: (i, 0, h))] + side_specs,
        out_shape=[jax.ShapeDtypeStruct((b, s, n_groups * LANES), BF16)]
        + [jax.ShapeDtypeStruct(w.shape, BF16) for w in side_casts],
        scratch_shapes=[
            *pick_scratch,
            pltpu.VMEM((t, t), F32),
            pltpu.VMEM((t, t), F32),
            pltpu.VMEM((t, t), F32),
            pltpu.VMEM((t, t), F32),
            pltpu.VMEM((2, 2, 1, t), F32),
            pltpu.VMEM((nt, 2, 1, t), F32),
            pltpu.VMEM((nt, 2, dv + ONES_ROWS, t), F32),
        ],
        compiler_params=_params("parallel", "parallel"),
        name=mode + "_attn",
    )(tab, qt, k, vt, *extra, *side_casts)
    return outs if side_casts else outs[0]


FF_CHUNK = 512


def _post_kernel(*refs, n_mix, final_norm, next_proj):
    h_ref = refs[0]
    ys = refs[1:1 + n_mix]
    ws = refs[1 + n_mix:1 + 2 * n_mix]
    rest = refs[1 + 2 * n_mix:]
    gx_ref, qk_ref, vo_ref, gm_ref, wu_ref, wd_ref, gf_ref = rest[:7]
    if next_proj:
        gn_ref, wn_ref, o_ref, nqt_ref, nk_ref, nvt_ref = rest[7:]
    else:
        (o_ref,) = rest[7:]

    h = h_ref[0]
    for y_ref, w_ref in zip(ys, ws):
        h = h + _dot(y_ref[0], w_ref[...])

    logits = _dot(_rms(h, gx_ref[...]).astype(BF16), qk_ref[...])
    probs = []
    for e in range(XATTN_HEADS):
        s = logits[:, e * N_MEM:(e + 1) * N_MEM]
        p = jnp.exp(s - jnp.max(s, axis=1, keepdims=True))
        probs.append((p / jnp.sum(p, axis=1, keepdims=True)).astype(BF16))
    h = h + _dot(jnp.concatenate(probs, axis=1), vo_ref[...])

    xn = _rms(h, gm_ref[...]).astype(BF16)
    for c in range(wu_ref.shape[1] // FF_CHUNK):
        sl = slice(c * FF_CHUNK, (c + 1) * FF_CHUNK)
        u = jnp.maximum(_dot(xn, wu_ref[:, sl]), 0.0)
        h = h + _dot((u * u).astype(BF16), wd_ref[sl, :])
    if final_norm:
        h = _rms(h, gf_ref[...])
    o_ref[0] = h
    if next_proj:
        _odd_proj_rows(h, gn_ref, wn_ref, nqt_ref, nk_ref, nvt_ref)


def _layer_spec(shape, layer, block=0):
    return pl.BlockSpec((None,) + tuple(shape), lambda *_: (layer, block, 0))


def _post(h, ys, w_mix, mix_layer, gx, qk, vo, gm, wu, wd, layer, gf, final_norm, next_proj=None):
    b, s, d = h.shape
    tm = ROW_TILE
    n_mix = len(ys)
    widths = [y.shape[2] for y in ys]
    assert all(wd_ == widths[0] for wd_ in widths) and sum(widths) == w_mix.shape[1]
    tile = lambda width: pl.BlockSpec((1, tm, width), lambda i, j: (i, j, 0))
    in_specs = [tile(d)] + [tile(wd_) for wd_ in widths]
    in_specs += [_layer_spec((widths[0], d), mix_layer, block=k) for k in range(n_mix)]
    in_specs += [
        _const_spec((1, d)),
        pl.BlockSpec((None, None, d, d), lambda i, j: (layer, i, 0, 0)),
        pl.BlockSpec((None, None, d, d), lambda i, j: (layer, i, 0, 0)),
        _const_spec((1, d)),
        _layer_spec(wu.shape[1:], layer),
        _layer_spec(wd.shape[1:], layer),
        _const_spec((1, d)),
    ]
    out_specs, out_shape, extra_in = [tile(d)], [jax.ShapeDtypeStruct((b, s, d), F32)], []
    if next_proj:
        in_specs += [_const_spec((1, d)), _const_spec(next_proj[1].shape)]
        extra_in = list(next_proj)
        qkv_specs, qkv_shapes = _qkv_specs(b, s, tm, DIFF_HEADS)
        out_specs += qkv_specs
        out_shape += qkv_shapes
    outs = pl.pallas_call(
        functools.partial(_post_kernel, n_mix=n_mix, final_norm=final_norm, next_proj=bool(next_proj)),
        grid=(b, s // tm),
        in_specs=in_specs,
        out_specs=out_specs,
        out_shape=out_shape,
        compiler_params=_params("parallel", "parallel"),
        name="post",
    )(h, *ys, *([w_mix] * n_mix), gx, qk, vo, gm, wu, wd, gf, *extra_in)
    return outs if next_proj else outs[0]


def _fold_xattn_kernel(mem_ref, g_ref, wkv_ref, wq_ref, wo_ref, qk_ref, vo_ref):
    d = wq_ref.shape[0]
    dh = d // XATTN_HEADS
    kv = _dot(_rms(mem_ref[...], g_ref[...]).astype(BF16), wkv_ref[...]).astype(BF16)
    for e in range(XATTN_HEADS):
        sl = slice(e * dh, (e + 1) * dh)
        qk_ref[:, e * N_MEM:(e + 1) * N_MEM] = (_dot_nt(wq_ref[:, sl], kv[:, sl]) * dh ** -0.5).astype(BF16)
        vo_ref[e * N_MEM:(e + 1) * N_MEM, :] = _dot(kv[:, d + e * dh:d + (e + 1) * dh], wo_ref[sl, :]).astype(BF16)


def _fold_xattn(mem, g, wkv, wq, wo):
    b, n_mem, d = mem.shape
    depth = wkv.shape[0]
    assert n_mem == N_MEM and XATTN_HEADS * N_MEM == d
    per_layer = lambda cols: pl.BlockSpec((None, d, cols), lambda l, i: (l, 0, 0))
    per_batch = pl.BlockSpec((None, None, d, d), lambda l, i: (l, i, 0, 0))
    return pl.pallas_call(
        _fold_xattn_kernel,
        grid=(depth, b),
        in_specs=[
            pl.BlockSpec((None, N_MEM, d), lambda l, i: (i, 0, 0)),
            _const_spec((1, d)),
            per_layer(2 * d),
            per_layer(d),
            per_layer(d),
        ],
        out_specs=[per_batch, per_batch],
        out_shape=[jax.ShapeDtypeStruct((depth, b, d, d), BF16)] * 2,
        compiler_params=_params("parallel", "parallel"),
        name="fold_xattn",
    )(mem, g, wkv, wq, wo)


def _block_diag(w):
    g, n, _ = w.shape
    eye = jnp.eye(g, dtype=w.dtype)
    return (eye[:, None, :, None] * w[:, :, None, :]).reshape(g * n, g * n)


def kernel(x, mem, g_mem, g_final, mix_norm_g, xattn_norm_g, mlp_norm_g, w_in_even, conv_w, conv_b, w_rgate, b_rgate, w_igate, b_igate, lru_lambda, fox_forget_b, w_out_even, w_in_odd, lambda_q1, lambda_k1, lambda_q2, lambda_k2, diff_norm_g, w_out_odd, xattn_wq, xattn_wkv, xattn_wo, w_up, w_down):
    b, s, d = x.shape
    depth = mix_norm_g.shape[0]
    row = lambda v: v.reshape(1, -1).astype(F32)

    wu_all = wd_all = None
    w_out_even_all, w_out_odd_all = w_out_even.astype(BF16), w_out_odd.astype(BF16)
    qk_all, vo_all = _fold_xattn(mem, row(g_mem), xattn_wkv.astype(BF16),
                                 xattn_wq.astype(BF16), xattn_wo.astype(BF16))

    h = x
    for layer in range(depth):
        i = layer // 2
        if layer % 2 == 0:
            w_in = jnp.pad(w_in_even[i], ((0, 0), (0, 2688 - w_in_even.shape[2]))).astype(BF16)
            zl, qt, k, vt, kx = _even_proj(h, row(mix_norm_g[layer]), w_in,
                                           fox_forget_b[i].reshape(FOX_HEADS, 1).astype(F32))
            wg = jnp.concatenate([_block_diag(w_rgate[i]), _block_diag(w_igate[i])], axis=1).astype(BF16)
            bg = jnp.concatenate([b_rgate[i], b_igate[i]]).reshape(1, -1).astype(F32)
            y_lru = _lru(zl, conv_w[i].astype(F32), row(conv_b[i]), wg, bg, row(lru_lambda[i]))
            if wu_all is None:
                slabs = b * (FOX_HEADS // 2)
                y_fox, wu_all, wd_all = _attention(
                    "fox", qt, k, vt, (kx,),
                    side_casts=(w_up.reshape(slabs, -1, w_up.shape[2]), w_down.reshape(slabs, -1, w_down.shape[2])))
                wu_all, wd_all = wu_all.reshape(w_up.shape), wd_all.reshape(w_down.shape)
            else:
                y_fox = _attention("fox", qt, k, vt, (kx,))
            ys, w_mix = [y_lru, y_fox], w_out_even_all
        else:
            qt, k, vt = odd_inputs
            lambda_init = 0.8 - 0.6 * math.exp(-0.3 * layer)
            lamv = jnp.stack([lambda_q1[i], lambda_k1[i], lambda_q2[i], lambda_k2[i]]).astype(F32)
            y = _attention("diff", qt, k, vt, (lamv, row(diff_norm_g[i])), lambda_init=lambda_init)
            ys, w_mix = [y], w_out_odd_all

        next_proj = None
        if layer % 2 == 0 and layer + 1 < depth:
            next_proj = (row(mix_norm_g[layer + 1]), w_in_odd[(layer + 1) // 2].astype(BF16))
        res = _post(h, ys, w_mix, i, row(xattn_norm_g[layer]), qk_all, vo_all,
                    row(mlp_norm_g[layer]), wu_all, wd_all, layer,
                    row(g_final), final_norm=(layer == depth - 1), next_proj=next_proj)
        if next_proj:
            h, *odd_inputs = res
        else:
            h = res
    return h
```

```python
import functools
import math

import jax
import jax.numpy as jnp
from jax import lax
from jax.experimental import pallas as pl
from jax.experimental.pallas import tpu as pltpu

F32 = jnp.float32
BF16 = jnp.bfloat16

LANES = 128
SUBLANES = 8
VMEM_LIMIT = 56 * 1024 * 1024

EPS = 1e-6
N_MEM = 256
LRU_WIDTH = 512
LRU_BLOCKS = 8
CONV_WIDTH = 4
LRU_C = 8.0
HEAD_DIM = 64
FOX_HEADS = 8
DIFF_HEADS = 8
XATTN_HEADS = 4

LOG2E = 1.4426950408889634
QK_SCALE = HEAD_DIM ** -0.5 * LOG2E
ONES_ROWS = 16

ROW_TILE = 512
PROJ_TILE = 1024
ATTN_TILE = 512
ITEMS_PER_TRIP = 14
LRU_TILE = 1024


def _params(*sem):
    return pltpu.CompilerParams(dimension_semantics=sem, vmem_limit_bytes=VMEM_LIMIT)


def _rms(x, g):
    return x * lax.rsqrt(jnp.mean(x * x, axis=-1, keepdims=True) + EPS) * g


def _dot(a, b):
    return jnp.dot(a, b, preferred_element_type=F32)


def _dot_nt(a, b):
    return lax.dot_general(a, b, (((1,), (1,)), ((), ())), preferred_element_type=F32)


def _log_sigmoid(x):
    return jnp.minimum(x, 0.0) - jnp.log1p(jnp.exp(-jnp.abs(x)))


def _sigmoid(x):
    return 0.5 * jnp.tanh(0.5 * x) + 0.5


def _expm1_given_exp(x, u):
    near = jnp.where(u == 1.0, x, (u - 1.0) * x / jnp.log(u))
    return jnp.where(jnp.abs(x) < 0.5, near, u - 1.0)


def _const_spec(shape):
    nd = len(shape)
    return pl.BlockSpec(shape, lambda *_: (0,) * nd)


def _even_proj_kernel(x_ref, g_ref, w_ref, fb_ref, zl_ref, qt_ref, k_ref, vt_ref, kx_ref, carry_ref):
    si = pl.program_id(1)
    tm = x_ref.shape[1]
    xn = _rms(x_ref[0], g_ref[...]).astype(BF16)
    fl = _dot(xn, w_ref[:, 2560:2688])
    lf = _log_sigmoid(fl.T[0:FOX_HEADS, :] + fb_ref[...])

    @pl.when(si == 0)
    def _():
        carry_ref[...] = jnp.zeros_like(carry_ref)

    pos = lax.broadcasted_iota(jnp.int32, lf.shape, 1)
    shift = 1
    while shift < tm:
        lf = lf + jnp.where(pos >= shift, pltpu.roll(lf, shift, 1), 0.0)
        shift *= 2
    cum = lf + carry_ref[:, 0:1]
    carry_ref[...] = jnp.broadcast_to(cum[:, tm - 1:tm], carry_ref.shape)

    c = cum * (-LOG2E)
    hi = c.astype(BF16).astype(F32)
    mid = (c - hi).astype(BF16).astype(F32)
    lo = (c - hi - mid).astype(BF16).astype(F32)
    pieces = jnp.concatenate([hi, mid, lo, jnp.zeros((LANES - 3 * FOX_HEADS, tm), F32)], axis=0)
    kx_ref[0] = pieces.T.astype(BF16)

    zl_ref[0] = _dot(xn, w_ref[:, 0:1024])
    _store_group_transposed(qt_ref, _dot(xn, w_ref[:, 1024:1536]) * QK_SCALE)
    _store_groups(k_ref, _dot(xn, w_ref[:, 1536:2048]))
    _store_group_transposed(vt_ref, _dot(xn, w_ref[:, 2048:2560]))


def _store_groups(ref, z):
    for grp in range(z.shape[1] // LANES):
        ref[0, grp] = z[:, grp * LANES:(grp + 1) * LANES].astype(ref.dtype)


def _store_group_transposed(ref, z):
    zt = z.T
    for grp in range(z.shape[1] // LANES):
        ref[0, grp] = zt[grp * LANES:(grp + 1) * LANES].astype(ref.dtype)


def _qkv_specs(b, s, tm, n_groups):
    specs = [
        pl.BlockSpec((1, n_groups, LANES, tm), lambda i, j: (i, 0, 0, j)),
        pl.BlockSpec((1, n_groups, tm, LANES), lambda i, j: (i, 0, j, 0)),
        pl.BlockSpec((1, n_groups, LANES, tm), lambda i, j: (i, 0, 0, j)),
    ]
    shapes = [
        jax.ShapeDtypeStruct((b, n_groups, LANES, s), BF16),
        jax.ShapeDtypeStruct((b, n_groups, s, LANES), BF16),
        jax.ShapeDtypeStruct((b, n_groups, LANES, s), BF16),
    ]
    return specs, shapes


def _even_proj(h, g, w, fb):
    b, s, d = h.shape
    tm = PROJ_TILE
    n = w.shape[1]
    qkv_specs, qkv_shapes = _qkv_specs(b, s, tm, FOX_HEADS // 2)
    return pl.pallas_call(
        _even_proj_kernel,
        grid=(b, s // tm),
        in_specs=[
            pl.BlockSpec((1, tm, d), lambda i, j: (i, j, 0)),
            _const_spec((1, d)),
            _const_spec((d, n)),
            _const_spec((FOX_HEADS, 1)),
        ],
        out_specs=[pl.BlockSpec((1, tm, 1024), lambda i, j: (i, j, 0)), *qkv_specs,
                   pl.BlockSpec((1, tm, LANES), lambda i, j: (i, j, 0))],
        out_shape=[jax.ShapeDtypeStruct((b, s, 1024), F32), *qkv_shapes,
                   jax.ShapeDtypeStruct((b, s, LANES), BF16)],
        scratch_shapes=[pltpu.VMEM((FOX_HEADS, LANES), F32)],
        compiler_params=_params("parallel", "arbitrary"),
        name="even_proj",
    )(h, g, w, fb)


def _odd_proj_rows(h, g_ref, w_ref, qt_ref, k_ref, vt_ref):
    n = w_ref.shape[1] // 3
    xn = _rms(h, g_ref[...]).astype(BF16)
    _store_group_transposed(qt_ref, _dot(xn, w_ref[:, 0:n]) * QK_SCALE)
    _store_groups(k_ref, _dot(xn, w_ref[:, n:2 * n]))
    _store_group_transposed(vt_ref, _dot(xn, w_ref[:, 2 * n:3 * n]))


def _gelu_tanh(x):
    return 0.5 * x * (1.0 + jnp.tanh(math.sqrt(2.0 / math.pi) * (x + 0.044715 * (x * x * x))))


def _lru_kernel(xb_ref, gb_ref, cw_ref, cb_ref, wg_ref, bg_ref, lam_ref, y_ref,
                xbuf, a_s, u_s, hc):
    ti = pl.program_id(1)
    ts = xb_ref.shape[1]
    w = LRU_WIDTH

    @pl.when(ti == 0)
    def _():
        xbuf[0:SUBLANES, :] = jnp.zeros((SUBLANES, w), F32)
        hc[...] = jnp.zeros_like(hc)

    xbuf[SUBLANES:SUBLANES + ts, :] = xb_ref[0]
    xfull = xbuf[...]
    xc = cb_ref[...]
    for k in range(CONV_WIDTH):
        lag = CONV_WIDTH - 1 - k
        shifted = pltpu.roll(xfull, lag, 0) if lag else xfull
        xc = xc + shifted[SUBLANES:SUBLANES + ts] * cw_ref[k:k + 1, :]
    xbuf[0:SUBLANES, :] = xbuf[ts:ts + SUBLANES, :]

    gates = _dot(xc.astype(BF16), wg_ref[...]) + bg_ref[...]
    r = _sigmoid(gates[:, 0:w])
    i = _sigmoid(gates[:, w:2 * w])
    log_a = LRU_C * r * _log_sigmoid(lam_ref[...])
    a = jnp.exp(log_a)
    a_s[...] = a
    v = -_expm1_given_exp(2.0 * log_a, a * a)
    sqrt_v = jnp.where(v > 0.0, v * lax.rsqrt(v), 0.0)
    u_s[...] = sqrt_v * (i * xc)

    row = lax.broadcasted_iota(jnp.int32, (SUBLANES, w), 0)

    def group(gidx, carry):
        off = pl.multiple_of(gidx * SUBLANES, SUBLANES)
        a = a_s[pl.ds(off, SUBLANES), :]
        u = u_s[pl.ds(off, SUBLANES), :]
        for sh in (1, 2, 4):
            keep = row >= sh
            u = u + a * jnp.where(keep, pltpu.roll(u, sh, 0), 0.0)
            a = a * jnp.where(keep, pltpu.roll(a, sh, 0), 1.0)
        hgrp = u + a * carry
        u_s[pl.ds(off, SUBLANES), :] = hgrp
        return jnp.broadcast_to(hgrp[SUBLANES - 1:SUBLANES, :], (SUBLANES, w))

    hc[...] = lax.fori_loop(0, ts // SUBLANES, group, hc[...], unroll=8)
    y_ref[0] = (u_s[...] * _gelu_tanh(gb_ref[0])).astype(y_ref.dtype)


def _lru(zl, cw, cb, wg, bg, lam):
    b, s, _ = zl.shape
    ts = LRU_TILE
    w = LRU_WIDTH
    return pl.pallas_call(
        _lru_kernel,
        grid=(b, s // ts),
        in_specs=[
            pl.BlockSpec((1, ts, w), lambda i, j: (i, j, 0)),
            pl.BlockSpec((1, ts, w), lambda i, j: (i, j, 1)),
            _const_spec((CONV_WIDTH, w)),
            _const_spec((1, w)),
            _const_spec((w, 2 * w)),
            _const_spec((1, 2 * w)),
            _const_spec((1, w)),
        ],
        out_specs=pl.BlockSpec((1, ts, w), lambda i, j: (i, j, 0)),
        out_shape=jax.ShapeDtypeStruct((b, s, w), BF16),
        scratch_shapes=[
            pltpu.VMEM((SUBLANES + ts, w), F32),
            pltpu.VMEM((ts, w), F32),
            pltpu.VMEM((ts, w), F32),
            pltpu.VMEM((SUBLANES, w), F32),
        ],
        compiler_params=_params("parallel", "arbitrary"),
        name="rglru",
    )(zl, zl, cw, cb, wg, bg, lam)


def _attn_items(nt):
    items = [(qi, j) for qi in range(1, nt) for j in range(qi)]
    return items + [items[-1]] * 2


def _attn_kernel(*refs, mode, lambda_init, n_side):
    n_in = 5 if mode == "fox" else 6
    side_in = refs[n_in:n_in + n_side]
    side_out = refs[n_in + n_side + 1:n_in + 2 * n_side + 1]
    refs = refs[:n_in] + refs[n_in + n_side:n_in + n_side + 1] + refs[n_in + 2 * n_side + 1:]
    if mode == "fox":
        tab_ref, qt_ref, k_ref, vt_ref, kx_ref, o_ref, pk_s, sa0, sa1, sb0, sb1, mc_s, m_s, acc_s = refs
    else:
        tab_ref, qt_ref, k_ref, vt_ref, lamv_ref, g_ref, o_ref, sa0, sa1, sb0, sb1, mc_s, m_s, acc_s = refs
    bufs = ((sa0, sa1), (sb0, sb1))
    hp = pl.program_id(1)
    t = ATTN_TILE
    s_len = k_ref.shape[2]
    nt = s_len // t
    dv = HEAD_DIM if mode == "fox" else LANES

    if mode == "fox":
        rowi = lax.broadcasted_iota(jnp.int32, (LANES, t), 0)
        for e in range(2):
            head = 2 * hp + e
            pick = (rowi == head) | (rowi == head + FOX_HEADS) | (rowi == head + 2 * FOX_HEADS)
            pk_s[e] = jnp.where(pick, 1.0, 0.0).astype(BF16)

    def q_weights(qi, e):
        qt = qt_ref[0, 0, :, pl.ds(pl.multiple_of(qi * t, t), t)]
        zero = jnp.zeros((HEAD_DIM, t), BF16)
        parts = [qt[0:HEAD_DIM], zero] if e == 0 else [zero, qt[HEAD_DIM:LANES]]
        if mode == "fox":
            parts.append(pk_s[e])
        return jnp.concatenate(parts, axis=0)

    def v_rows(e, off):
        rows = slice(e * dv, (e + 1) * dv) if mode == "fox" else slice(0, dv)
        return jnp.concatenate([vt_ref[0, 0, rows, pl.ds(off, t)], jnp.ones((ONES_ROWS, t), BF16)], axis=0)

    def colmax(x):
        acc = x[0:16]
        for i in range(1, x.shape[0] // 16):
            acc = jnp.maximum(acc, x[i * 16:(i + 1) * 16])
        return jnp.max(acc, axis=0, keepdims=True)

    def key_block(off):
        kb = k_ref[0, 0, pl.ds(off, t), :]
        if mode == "fox":
            kb = jnp.concatenate([kb, kx_ref[0, pl.ds(off, t), :]], axis=1)
        return kb

    def stage1(qi, j, par, e):
        s = _dot(key_block(pl.multiple_of(j * t, t)), q_weights(qi, e))
        bufs[par][e][...] = s
        mc_s[par, e] = colmax(s)

    def stage2(qi, j, par, e):
        off = pl.multiple_of(j * t, t)
        vt = v_rows(e, off)
        m_prev = m_s[qi, e]
        m_new = jnp.maximum(m_prev, mc_s[par, e])
        alpha = jnp.exp2(m_prev - m_new)
        p = jnp.exp2(bufs[par][e][...] - m_new).astype(BF16)
        acc_s[qi, e] = alpha * acc_s[qi, e] + _dot(vt, p)
        m_s[qi, e] = m_new

    half = t // 2

    def stage1_diag(qi, par, e):
        kb = key_block(qi * t)
        w = q_weights(qi, e)
        keys = lax.broadcasted_iota(jnp.int32, (half, half), 0)
        qpos = lax.broadcasted_iota(jnp.int32, (half, half), 1)
        tri = keys <= qpos
        s_left = jnp.where(tri, _dot(kb[0:half], w[:, 0:half]), -jnp.inf)
        s_right = _dot(kb, w[:, half:t])
        s_top = s_right[0:half]
        s_bot = jnp.where(tri, s_right[half:t], -jnp.inf)
        buf = bufs[par][e]
        buf[0:half, 0:half] = s_left
        buf[0:half, half:t] = s_top
        buf[half:t, half:t] = s_bot
        mc_s[par, e] = jnp.concatenate(
            [colmax(s_left), jnp.maximum(colmax(s_top), colmax(s_bot))], axis=1)

    def stage2_diag(qi, par, e):
        buf = bufs[par][e]
        vt = v_rows(e, qi * t)
        m_prev = m_s[qi, e]
        m_new = jnp.maximum(m_prev, mc_s[par, e])
        alpha = jnp.exp2(m_prev - m_new)
        p_left = jnp.exp2(buf[0:half, 0:half] - m_new[:, 0:half]).astype(BF16)
        p_right = jnp.exp2(buf[:, half:t] - m_new[:, half:t]).astype(BF16)
        upd = jnp.concatenate([_dot(vt[:, 0:half], p_left), _dot(vt, p_right)], axis=1)
        acc_s[qi, e] = alpha * acc_s[qi, e] + upd
        m_s[qi, e] = m_new

    def item(w):
        return tab_ref[w, 0], tab_ref[w, 1]

    m_s[...] = jnp.full(m_s.shape, -jnp.inf, F32)
    acc_s[...] = jnp.zeros(acc_s.shape, F32)

    qa, ja = item(0)
    stage1(qa, ja, 0, 0)
    stage1(qa, ja, 0, 1)

    def body(i, c):
        w = ITEMS_PER_TRIP * i
        for k in range(ITEMS_PER_TRIP):
            qc, jc = item(w + k)
            qn, jn = item(w + k + 1)
            for e in range(2):
                stage1(qn, jn, (k + 1) % 2, e)
                stage2(qc, jc, k % 2, e)
        return c

    lax.fori_loop(0, (nt * (nt - 1)) // (2 * ITEMS_PER_TRIP), body, 0)

    for src, dst in zip(side_in, side_out):
        dst[...] = src[...].astype(dst.dtype)

    stage1_diag(0, 0, 0)
    stage1_diag(0, 0, 1)
    for qi in range(nt):
        par = qi % 2
        for e in range(2):
            if qi + 1 < nt:
                stage1_diag(qi + 1, 1 - par, e)
            stage2_diag(qi, par, e)

    for qi in range(nt):
        o0 = acc_s[qi, 0, 0:dv] / acc_s[qi, 0, dv:dv + 1]
        o1 = acc_s[qi, 1, 0:dv] / acc_s[qi, 1, dv:dv + 1]
        if mode == "fox":
            o = jnp.concatenate([o0, o1], axis=0).T
        else:
            lv = lamv_ref[...]
            lam = (jnp.exp(jnp.sum(lv[0:1] * lv[1:2], axis=1, keepdims=True))
                   - jnp.exp(jnp.sum(lv[2:3] * lv[3:4], axis=1, keepdims=True)) + lambda_init)
            o = (o0 - lam * o1).T
            o = o * lax.rsqrt(jnp.mean(o * o, axis=-1, keepdims=True) + EPS)
            o = o * g_ref[...] * (1.0 - lambda_init)
        o_ref[0, qi * t:(qi + 1) * t, :] = o.astype(o_ref.dtype)


def _attention(mode, qt, k, vt, extra, lambda_init=0.0, side_casts=()):
    b, n_groups, _, s = qt.shape
    t = ATTN_TILE
    nt = s // t
    assert (nt * (nt - 1)) % (2 * ITEMS_PER_TRIP) == 0
    tab = jnp.asarray(_attn_items(nt), jnp.int32)
    in_specs = [
        pl.BlockSpec(memory_space=pltpu.SMEM),
        pl.BlockSpec((1, 1, LANES, s), lambda i, h: (i, h, 0, 0)),
        pl.BlockSpec((1, 1, s, LANES), lambda i, h: (i, h, 0, 0)),
        pl.BlockSpec((1, 1, LANES, s), lambda i, h: (i, h, 0, 0)),
    ]
    if mode == "fox":
        (kx,) = extra
        in_specs.append(pl.BlockSpec((1, s, LANES), lambda i, h: (i, 0, 0)))
        dv = HEAD_DIM
        pick_scratch = [pltpu.VMEM((2, LANES, t), BF16)]
    else:
        lamv, g = extra
        in_specs += [_const_spec(lamv.shape), _const_spec(g.shape)]
        dv = LANES
        pick_scratch = []
    side_specs = [pl.BlockSpec((1,) + w.shape[1:], lambda i, h: (i * n_groups + h, 0, 0)) for w in side_casts]
    assert all(w.shape[0] == b * n_groups for w in side_casts)
    outs = pl.pallas_call(
        functools.partial(_attn_kernel, mode=mode, lambda_init=lambda_init, n_side=len(side_casts)),
        grid=(b, n_groups),
        in_specs=in_specs + side_specs,
        out_specs=[pl.BlockSpec((1, s, LANES), lambda i, h: (i, 0, h))] + side_specs,
        out_shape=[jax.ShapeDtypeStruct((b, s, n_groups * LANES), BF16)]
        + [jax.ShapeDtypeStruct(w.shape, BF16) for w in side_casts],
        scratch_shapes=[
            *pick_scratch,
            pltpu.VMEM((t, t), F32),
            pltpu.VMEM((t, t), F32),
            pltpu.VMEM((t, t), F32),
            pltpu.VMEM((t, t), F32),
            pltpu.VMEM((2, 2, 1, t), F32),
            pltpu.VMEM((nt, 2, 1, t), F32),
            pltpu.VMEM((nt, 2, dv + ONES_ROWS, t), F32),
        ],
        compiler_params=_params("parallel", "parallel"),
        name=mode + "_attn",
    )(tab, qt, k, vt, *extra, *side_casts)
    return outs if side_casts else outs[0]


FF_CHUNK = 512


def _post_kernel(*refs, n_mix, final_norm, next_proj):
    h_ref = refs[0]
    ys = refs[1:1 + n_mix]
    ws = refs[1 + n_mix:1 + 2 * n_mix]
    rest = refs[1 + 2 * n_mix:]
    gx_ref, qk_ref, vo_ref, gm_ref, wu_ref, wd_ref, gf_ref = rest[:7]
    if next_proj:
        gn_ref, wn_ref, o_ref, nqt_ref, nk_ref, nvt_ref = rest[7:]
    else:
        (o_ref,) = rest[7:]

    h = h_ref[0]
    for y_ref, w_ref in zip(ys, ws):
        h = h + _dot(y_ref[0], w_ref[...])

    logits = _dot(_rms(h, gx_ref[...]).astype(BF16), qk_ref[...])
    probs = []
    for e in range(XATTN_HEADS):
        s = logits[:, e * N_MEM:(e + 1) * N_MEM]
        p = jnp.exp(s - jnp.max(s, axis=1, keepdims=True))
        probs.append((p / jnp.sum(p, axis=1, keepdims=True)).astype(BF16))
    h = h + _dot(jnp.concatenate(probs, axis=1), vo_ref[...])

    xn = _rms(h, gm_ref[...]).astype(BF16)
    for c in range(wu_ref.shape[1] // FF_CHUNK):
        sl = slice(c * FF_CHUNK, (c + 1) * FF_CHUNK)
        u = jnp.maximum(_dot(xn, wu_ref[:, sl]), 0.0)
        h = h + _dot((u * u).astype(BF16), wd_ref[sl, :])
    if final_norm:
        h = _rms(h, gf_ref[...])
    o_ref[0] = h
    if next_proj:
        _odd_proj_rows(h, gn_ref, wn_ref, nqt_ref, nk_ref, nvt_ref)


def _layer_spec(shape, layer, block=0):
    return pl.BlockSpec((None,) + tuple(shape), lambda *_: (layer, block, 0))


def _post(h, ys, w_mix, mix_layer, gx, qk, vo, gm, wu, wd, layer, gf, final_norm, next_proj=None):
    b, s, d = h.shape
    tm = ROW_TILE
    n_mix = len(ys)
    widths = [y.shape[2] for y in ys]
    assert all(wd_ == widths[0] for wd_ in widths) and sum(widths) == w_mix.shape[1]
    tile = lambda width: pl.BlockSpec((1, tm, width), lambda i, j: (i, j, 0))
    in_specs = [tile(d)] + [tile(wd_) for wd_ in widths]
    in_specs += [_layer_spec((widths[0], d), mix_layer, block=k) for k in range(n_mix)]
    in_specs += [
        _const_spec((1, d)),
        pl.BlockSpec((None, None, d, d), lambda i, j: (layer, i, 0, 0)),
        pl.BlockSpec((None, None, d, d), lambda i, j: (layer, i, 0, 0)),
        _const_spec((1, d)),
        _layer_spec(wu.shape[1:], layer),
        _layer_spec(wd.shape[1:], layer),
        _const_spec((1, d)),
    ]
    out_specs, out_shape, extra_in = [tile(d)], [jax.ShapeDtypeStruct((b, s, d), F32)], []
    if next_proj:
        in_specs += [_const_spec((1, d)), _const_spec(next_proj[1].shape)]
        extra_in = list(next_proj)
        qkv_specs, qkv_shapes = _qkv_specs(b, s, tm, DIFF_HEADS)
        out_specs += qkv_specs
        out_shape += qkv_shapes
    outs = pl.pallas_call(
        functools.partial(_post_kernel, n_mix=n_mix, final_norm=final_norm, next_proj=bool(next_proj)),
        grid=(b, s // tm),
        in_specs=in_specs,
        out_specs=out_specs,
        out_shape=out_shape,
        compiler_params=_params("parallel", "parallel"),
        name="post",
    )(h, *ys, *([w_mix] * n_mix), gx, qk, vo, gm, wu, wd, gf, *extra_in)
    return outs if next_proj else outs[0]


def _fold_xattn_kernel(mem_ref, g_ref, wkv_ref, wq_ref, wo_ref, qk_ref, vo_ref):
    d = wq_ref.shape[0]
    dh = d // XATTN_HEADS
    kv = _dot(_rms(mem_ref[...], g_ref[...]).astype(BF16), wkv_ref[...]).astype(BF16)
    for e in range(XATTN_HEADS):
        sl = slice(e * dh, (e + 1) * dh)
        qk_ref[:, e * N_MEM:(e + 1) * N_MEM] = (_dot_nt(wq_ref[:, sl], kv[:, sl]) * dh ** -0.5).astype(BF16)
        vo_ref[e * N_MEM:(e + 1) * N_MEM, :] = _dot(kv[:, d + e * dh:d + (e + 1) * dh], wo_ref[sl, :]).astype(BF16)


def _fold_xattn(mem, g, wkv, wq, wo):
    b, n_mem, d = mem.shape
    depth = wkv.shape[0]
    assert n_mem == N_MEM and XATTN_HEADS * N_MEM == d
    per_layer = lambda cols: pl.BlockSpec((None, d, cols), lambda l, i: (l, 0, 0))
    per_batch = pl.BlockSpec((None, None, d, d), lambda l, i: (l, i, 0, 0))
    return pl.pallas_call(
        _fold_xattn_kernel,
        grid=(depth, b),
        in_specs=[
            pl.BlockSpec((None, N_MEM, d), lambda l, i: (i, 0, 0)),
            _const_spec((1, d)),
            per_layer(2 * d),
            per_layer(d),
            per_layer(d),
        ],
        out_specs=[per_batch, per_batch],
        out_shape=[jax.ShapeDtypeStruct((depth, b, d, d), BF16)] * 2,
        compiler_params=_params("parallel", "parallel"),
        name="fold_xattn",
    )(mem, g, wkv, wq, wo)


def _block_diag(w):
    g, n, _ = w.shape
    eye = jnp.eye(g, dtype=w.dtype)
    return (eye[:, None, :, None] * w[:, :, None, :]).reshape(g * n, g * n)


def kernel(x, mem, g_mem, g_final, mix_norm_g, xattn_norm_g, mlp_norm_g, w_in_even, conv_w, conv_b, w_rgate, b_rgate, w_igate, b_igate, lru_lambda, fox_forget_b, w_out_even, w_in_odd, lambda_q1, lambda_k1, lambda_q2, lambda_k2, diff_norm_g, w_out_odd, xattn_wq, xattn_wkv, xattn_wo, w_up, w_down):
    b, s, d = x.shape
    depth = mix_norm_g.shape[0]
    row = lambda v: v.reshape(1, -1).astype(F32)

    late_weights = (w_up, w_down, w_out_even, w_out_odd, w_in_odd, xattn_wq, xattn_wo, xattn_wkv)
    late_bf16 = None

    h = x
    for layer in range(depth):
        i = layer // 2
        if layer % 2 == 0:
            w_in = jnp.pad(w_in_even[i], ((0, 0), (0, 2688 - w_in_even.shape[2]))).astype(BF16)
            zl, qt, k, vt, kx = _even_proj(h, row(mix_norm_g[layer]), w_in,
                                           fox_forget_b[i].reshape(FOX_HEADS, 1).astype(F32))
            wg = jnp.concatenate([_block_diag(w_rgate[i]), _block_diag(w_igate[i])], axis=1).astype(BF16)
            bg = jnp.concatenate([b_rgate[i], b_igate[i]]).reshape(1, -1).astype(F32)
            y_lru = _lru(zl, conv_w[i].astype(F32), row(conv_b[i]), wg, bg, row(lru_lambda[i]))
            if late_bf16 is None:
                slabs = b * (FOX_HEADS // 2)
                y_fox, *cast = _attention("fox", qt, k, vt, (kx,),
                                          side_casts=tuple(w.reshape(slabs, -1, w.shape[-1]) for w in late_weights))
                late_bf16 = [c.reshape(w.shape) for c, w in zip(cast, late_weights)]
                wu_all, wd_all, w_out_even_all, w_out_odd_all, w_in_odd_all, wq_all, wo_all, wkv_all = late_bf16
                qk_all, vo_all = _fold_xattn(mem, row(g_mem), wkv_all, wq_all, wo_all)
            else:
                y_fox = _attention("fox", qt, k, vt, (kx,))
            ys, w_mix = [y_lru, y_fox], w_out_even_all
        else:
            qt, k, vt = odd_inputs
            lambda_init = 0.8 - 0.6 * math.exp(-0.3 * layer)
            lamv = jnp.stack([lambda_q1[i], lambda_k1[i], lambda_q2[i], lambda_k2[i]]).astype(F32)
            y = _attention("diff", qt, k, vt, (lamv, row(diff_norm_g[i])), lambda_init=lambda_init)
            ys, w_mix = [y], w_out_odd_all

        next_proj = None
        if layer % 2 == 0 and layer + 1 < depth:
            next_proj = (row(mix_norm_g[layer + 1]), w_in_odd_all[(layer + 1) // 2])
        res = _post(h, ys, w_mix, i, row(xattn_norm_g[layer]), qk_all, vo_all,
                    row(mlp_norm_g[layer]), wu_all, wd_all, layer,
                    row(g_final), final_norm=(layer == depth - 1), next_proj=next_proj)
        if next_proj:
            h, *odd_inputs = res
        else:
            h = res
    return h
```

```python
import functools
import math

import jax
import jax.numpy as jnp
from jax import lax
from jax.experimental import pallas as pl
from jax.experimental.pallas import tpu as pltpu

F32 = jnp.float32
BF16 = jnp.bfloat16

LANES = 128
SUBLANES = 8
VMEM_LIMIT = 56 * 1024 * 1024

EPS = 1e-6
N_MEM = 256
LRU_WIDTH = 512
LRU_BLOCKS = 8
CONV_WIDTH = 4
LRU_C = 8.0
HEAD_DIM = 64
FOX_HEADS = 8
DIFF_HEADS = 8
XATTN_HEADS = 4

LOG2E = 1.4426950408889634
QK_SCALE = HEAD_DIM ** -0.5 * LOG2E
ONES_ROWS = 16

ROW_TILE = 512
PROJ_TILE = 1024
ATTN_TILE = 512
ITEMS_PER_TRIP = 14
LRU_TILE = 1024


def _params(*sem):
    return pltpu.CompilerParams(dimension_semantics=sem, vmem_limit_bytes=VMEM_LIMIT)


def _rms(x, g):
    return x * lax.rsqrt(jnp.mean(x * x, axis=-1, keepdims=True) + EPS) * g


def _dot(a, b):
    return jnp.dot(a, b, preferred_element_type=F32)


def _dot_nt(a, b):
    return lax.dot_general(a, b, (((1,), (1,)), ((), ())), preferred_element_type=F32)


def _log_sigmoid(x):
    return jnp.minimum(x, 0.0) - jnp.log1p(jnp.exp(-jnp.abs(x)))


def _sigmoid(x):
    return 0.5 * jnp.tanh(0.5 * x) + 0.5


def _expm1_given_exp(x, u):
    near = jnp.where(u == 1.0, x, (u - 1.0) * x / jnp.log(u))
    return jnp.where(jnp.abs(x) < 0.5, near, u - 1.0)


def _const_spec(shape):
    nd = len(shape)
    return pl.BlockSpec(shape, lambda *_: (0,) * nd)


def _even_proj_kernel(x_ref, g_ref, w_ref, wf_ref, fb_ref, zl_ref, qt_ref, k_ref, vt_ref, kx_ref, carry_ref):
    si = pl.program_id(1)
    tm = x_ref.shape[1]
    xn = _rms(x_ref[0], g_ref[...]).astype(BF16)
    fl = _dot(xn, wf_ref[...])
    lf = _log_sigmoid(fl.T[0:FOX_HEADS, :] + fb_ref[...])

    @pl.when(si == 0)
    def _():
        carry_ref[...] = jnp.zeros_like(carry_ref)

    pos = lax.broadcasted_iota(jnp.int32, lf.shape, 1)
    shift = 1
    while shift < tm:
        lf = lf + jnp.where(pos >= shift, pltpu.roll(lf, shift, 1), 0.0)
        shift *= 2
    cum = lf + carry_ref[:, 0:1]
    carry_ref[...] = jnp.broadcast_to(cum[:, tm - 1:tm], carry_ref.shape)

    c = cum * (-LOG2E)
    hi = c.astype(BF16).astype(F32)
    mid = (c - hi).astype(BF16).astype(F32)
    lo = (c - hi - mid).astype(BF16).astype(F32)
    pieces = jnp.concatenate([hi, mid, lo, jnp.zeros((LANES - 3 * FOX_HEADS, tm), F32)], axis=0)
    kx_ref[0] = pieces.T.astype(BF16)

    zl_ref[0] = _dot(xn, w_ref[:, 0:1024])
    _store_group_transposed(qt_ref, _dot(xn, w_ref[:, 1024:1536]) * QK_SCALE)
    _store_groups(k_ref, _dot(xn, w_ref[:, 1536:2048]))
    _store_group_transposed(vt_ref, _dot(xn, w_ref[:, 2048:2560]))


def _store_groups(ref, z):
    for grp in range(z.shape[1] // LANES):
        ref[0, grp] = z[:, grp * LANES:(grp + 1) * LANES].astype(ref.dtype)


def _store_group_transposed(ref, z):
    zt = z.T
    for grp in range(z.shape[1] // LANES):
        ref[0, grp] = zt[grp * LANES:(grp + 1) * LANES].astype(ref.dtype)


def _qkv_specs(b, s, tm, n_groups):
    specs = [
        pl.BlockSpec((1, n_groups, LANES, tm), lambda i, j: (i, 0, 0, j)),
        pl.BlockSpec((1, n_groups, tm, LANES), lambda i, j: (i, 0, j, 0)),
        pl.BlockSpec((1, n_groups, LANES, tm), lambda i, j: (i, 0, 0, j)),
    ]
    shapes = [
        jax.ShapeDtypeStruct((b, n_groups, LANES, s), BF16),
        jax.ShapeDtypeStruct((b, n_groups, s, LANES), BF16),
        jax.ShapeDtypeStruct((b, n_groups, LANES, s), BF16),
    ]
    return specs, shapes


def _even_proj(h, g, w, wf, fb):
    b, s, d = h.shape
    tm = PROJ_TILE
    n = w.shape[1]
    qkv_specs, qkv_shapes = _qkv_specs(b, s, tm, FOX_HEADS // 2)
    return pl.pallas_call(
        _even_proj_kernel,
        grid=(b, s // tm),
        in_specs=[
            pl.BlockSpec((1, tm, d), lambda i, j: (i, j, 0)),
            _const_spec((1, d)),
            _const_spec((d, n)),
            _const_spec((d, LANES)),
            _const_spec((FOX_HEADS, 1)),
        ],
        out_specs=[pl.BlockSpec((1, tm, 1024), lambda i, j: (i, j, 0)), *qkv_specs,
                   pl.BlockSpec((1, tm, LANES), lambda i, j: (i, j, 0))],
        out_shape=[jax.ShapeDtypeStruct((b, s, 1024), F32), *qkv_shapes,
                   jax.ShapeDtypeStruct((b, s, LANES), BF16)],
        scratch_shapes=[pltpu.VMEM((FOX_HEADS, LANES), F32)],
        compiler_params=_params("parallel", "arbitrary"),
        name="even_proj",
    )(h, g, w, wf, fb)


def _odd_proj_rows(h, g_ref, w_ref, qt_ref, k_ref, vt_ref):
    n = w_ref.shape[1] // 3
    xn = _rms(h, g_ref[...]).astype(BF16)
    _store_group_transposed(qt_ref, _dot(xn, w_ref[:, 0:n]) * QK_SCALE)
    _store_groups(k_ref, _dot(xn, w_ref[:, n:2 * n]))
    _store_group_transposed(vt_ref, _dot(xn, w_ref[:, 2 * n:3 * n]))


def _gelu_tanh(x):
    return 0.5 * x * (1.0 + jnp.tanh(math.sqrt(2.0 / math.pi) * (x + 0.044715 * (x * x * x))))


def _lru_kernel(xb_ref, gb_ref, cw_ref, cb_ref, wg_ref, bg_ref, lam_ref, y_ref,
                xbuf, a_s, u_s, hc):
    ti = pl.program_id(1)
    ts = xb_ref.shape[1]
    w = LRU_WIDTH

    @pl.when(ti == 0)
    def _():
        xbuf[0:SUBLANES, :] = jnp.zeros((SUBLANES, w), F32)
        hc[...] = jnp.zeros_like(hc)

    xbuf[SUBLANES:SUBLANES + ts, :] = xb_ref[0]
    xfull = xbuf[...]
    xc = cb_ref[...]
    for k in range(CONV_WIDTH):
        lag = CONV_WIDTH - 1 - k
        shifted = pltpu.roll(xfull, lag, 0) if lag else xfull
        xc = xc + shifted[SUBLANES:SUBLANES + ts] * cw_ref[k:k + 1, :]
    xbuf[0:SUBLANES, :] = xbuf[ts:ts + SUBLANES, :]

    gates = _dot(xc.astype(BF16), wg_ref[...]) + bg_ref[...]
    r = _sigmoid(gates[:, 0:w])
    i = _sigmoid(gates[:, w:2 * w])
    log_a = LRU_C * r * _log_sigmoid(lam_ref[...])
    a = jnp.exp(log_a)
    a_s[...] = a
    v = -_expm1_given_exp(2.0 * log_a, a * a)
    sqrt_v = jnp.where(v > 0.0, v * lax.rsqrt(v), 0.0)
    u_s[...] = sqrt_v * (i * xc)

    row = lax.broadcasted_iota(jnp.int32, (SUBLANES, w), 0)

    def group(gidx, carry):
        off = pl.multiple_of(gidx * SUBLANES, SUBLANES)
        a = a_s[pl.ds(off, SUBLANES), :]
        u = u_s[pl.ds(off, SUBLANES), :]
        for sh in (1, 2, 4):
            keep = row >= sh
            u = u + a * jnp.where(keep, pltpu.roll(u, sh, 0), 0.0)
            a = a * jnp.where(keep, pltpu.roll(a, sh, 0), 1.0)
        hgrp = u + a * carry
        u_s[pl.ds(off, SUBLANES), :] = hgrp
        return jnp.broadcast_to(hgrp[SUBLANES - 1:SUBLANES, :], (SUBLANES, w))

    hc[...] = lax.fori_loop(0, ts // SUBLANES, group, hc[...], unroll=8)
    y_ref[0] = (u_s[...] * _gelu_tanh(gb_ref[0])).astype(y_ref.dtype)


def _lru(zl, cw, cb, wg, bg, lam):
    b, s, _ = zl.shape
    ts = LRU_TILE
    w = LRU_WIDTH
    return pl.pallas_call(
        _lru_kernel,
        grid=(b, s // ts),
        in_specs=[
            pl.BlockSpec((1, ts, w), lambda i, j: (i, j, 0)),
            pl.BlockSpec((1, ts, w), lambda i, j: (i, j, 1)),
            _const_spec((CONV_WIDTH, w)),
            _const_spec((1, w)),
            _const_spec((w, 2 * w)),
            _const_spec((1, 2 * w)),
            _const_spec((1, w)),
        ],
        out_specs=pl.BlockSpec((1, ts, w), lambda i, j: (i, j, 0)),
        out_shape=jax.ShapeDtypeStruct((b, s, w), BF16),
        scratch_shapes=[
            pltpu.VMEM((SUBLANES + ts, w), F32),
            pltpu.VMEM((ts, w), F32),
            pltpu.VMEM((ts, w), F32),
            pltpu.VMEM((SUBLANES, w), F32),
        ],
        compiler_params=_params("parallel", "arbitrary"),
        name="rglru",
    )(zl, zl, cw, cb, wg, bg, lam)


def _attn_items(nt):
    items = [(qi, j) for qi in range(1, nt) for j in range(qi)]
    return items + [items[-1]] * 2


def _attn_kernel(*refs, mode, lambda_init, n_side):
    n_in = 5 if mode == "fox" else 6
    side_in = refs[n_in:n_in + n_side]
    side_out = refs[n_in + n_side + 1:n_in + 2 * n_side + 1]
    refs = refs[:n_in] + refs[n_in + n_side:n_in + n_side + 1] + refs[n_in + 2 * n_side + 1:]
    if mode == "fox":
        tab_ref, qt_ref, k_ref, vt_ref, kx_ref, o_ref, pk_s, sa0, sa1, sb0, sb1, mc_s, m_s, acc_s = refs
    else:
        tab_ref, qt_ref, k_ref, vt_ref, lamv_ref, g_ref, o_ref, sa0, sa1, sb0, sb1, mc_s, m_s, acc_s = refs
    bufs = ((sa0, sa1), (sb0, sb1))
    hp = pl.program_id(1)
    t = ATTN_TILE
    s_len = k_ref.shape[2]
    nt = s_len // t
    dv = HEAD_DIM if mode == "fox" else LANES

    if mode == "fox":
        rowi = lax.broadcasted_iota(jnp.int32, (LANES, t), 0)
        for e in range(2):
            head = 2 * hp + e
            pick = (rowi == head) | (rowi == head + FOX_HEADS) | (rowi == head + 2 * FOX_HEADS)
            pk_s[e] = jnp.where(pick, 1.0, 0.0).astype(BF16)

    def q_weights(qi, e):
        qt = qt_ref[0, 0, :, pl.ds(pl.multiple_of(qi * t, t), t)]
        zero = jnp.zeros((HEAD_DIM, t), BF16)
        parts = [qt[0:HEAD_DIM], zero] if e == 0 else [zero, qt[HEAD_DIM:LANES]]
        if mode == "fox":
            parts.append(pk_s[e])
        return jnp.concatenate(parts, axis=0)

    def v_rows(e, off):
        rows = slice(e * dv, (e + 1) * dv) if mode == "fox" else slice(0, dv)
        return jnp.concatenate([vt_ref[0, 0, rows, pl.ds(off, t)], jnp.ones((ONES_ROWS, t), BF16)], axis=0)

    def colmax(x):
        acc = x[0:16]
        for i in range(1, x.shape[0] // 16):
            acc = jnp.maximum(acc, x[i * 16:(i + 1) * 16])
        return jnp.max(acc, axis=0, keepdims=True)

    def key_block(off):
        kb = k_ref[0, 0, pl.ds(off, t), :]
        if mode == "fox":
            kb = jnp.concatenate([kb, kx_ref[0, pl.ds(off, t), :]], axis=1)
        return kb

    def stage1(qi, j, par, e):
        s = _dot(key_block(pl.multiple_of(j * t, t)), q_weights(qi, e))
        bufs[par][e][...] = s
        mc_s[par, e] = colmax(s)

    def stage2(qi, j, par, e):
        off = pl.multiple_of(j * t, t)
        vt = v_rows(e, off)
        m_prev = m_s[qi, e]
        m_new = jnp.maximum(m_prev, mc_s[par, e])
        alpha = jnp.exp2(m_prev - m_new)
        p = jnp.exp2(bufs[par][e][...] - m_new).astype(BF16)
        acc_s[qi, e] = alpha * acc_s[qi, e] + _dot(vt, p)
        m_s[qi, e] = m_new

    half = t // 2

    def stage1_diag(qi, par, e):
        kb = key_block(qi * t)
        w = q_weights(qi, e)
        keys = lax.broadcasted_iota(jnp.int32, (half, half), 0)
        qpos = lax.broadcasted_iota(jnp.int32, (half, half), 1)
        tri = keys <= qpos
        s_left = jnp.where(tri, _dot(kb[0:half], w[:, 0:half]), -jnp.inf)
        s_right = _dot(kb, w[:, half:t])
        s_top = s_right[0:half]
        s_bot = jnp.where(tri, s_right[half:t], -jnp.inf)
        buf = bufs[par][e]
        buf[0:half, 0:half] = s_left
        buf[0:half, half:t] = s_top
        buf[half:t, half:t] = s_bot
        mc_s[par, e] = jnp.concatenate(
            [colmax(s_left), jnp.maximum(colmax(s_top), colmax(s_bot))], axis=1)

    def stage2_diag(qi, par, e):
        buf = bufs[par][e]
        vt = v_rows(e, qi * t)
        m_prev = m_s[qi, e]
        m_new = jnp.maximum(m_prev, mc_s[par, e])
        alpha = jnp.exp2(m_prev - m_new)
        p_left = jnp.exp2(buf[0:half, 0:half] - m_new[:, 0:half]).astype(BF16)
        p_right = jnp.exp2(buf[:, half:t] - m_new[:, half:t]).astype(BF16)
        upd = jnp.concatenate([_dot(vt[:, 0:half], p_left), _dot(vt, p_right)], axis=1)
        acc_s[qi, e] = alpha * acc_s[qi, e] + upd
        m_s[qi, e] = m_new

    def item(w):
        return tab_ref[w, 0], tab_ref[w, 1]

    m_s[...] = jnp.full(m_s.shape, -jnp.inf, F32)
    acc_s[...] = jnp.zeros(acc_s.shape, F32)

    qa, ja = item(0)
    stage1(qa, ja, 0, 0)
    stage1(qa, ja, 0, 1)

    def body(i, c):
        w = ITEMS_PER_TRIP * i
        for k in range(ITEMS_PER_TRIP):
            qc, jc = item(w + k)
            qn, jn = item(w + k + 1)
            for e in range(2):
                stage1(qn, jn, (k + 1) % 2, e)
                stage2(qc, jc, k % 2, e)
        return c

    lax.fori_loop(0, (nt * (nt - 1)) // (2 * ITEMS_PER_TRIP), body, 0)

    for src, dst in zip(side_in, side_out):
        dst[...] = src[...].astype(dst.dtype)

    stage1_diag(0, 0, 0)
    stage1_diag(0, 0, 1)
    for qi in range(nt):
        par = qi % 2
        for e in range(2):
            if qi + 1 < nt:
                stage1_diag(qi + 1, 1 - par, e)
            stage2_diag(qi, par, e)

    for qi in range(nt):
        o0 = acc_s[qi, 0, 0:dv] / acc_s[qi, 0, dv:dv + 1]
        o1 = acc_s[qi, 1, 0:dv] / acc_s[qi, 1, dv:dv + 1]
        if mode == "fox":
            o = jnp.concatenate([o0, o1], axis=0).T
        else:
            lv = lamv_ref[...]
            lam = (jnp.exp(jnp.sum(lv[0:1] * lv[1:2], axis=1, keepdims=True))
                   - jnp.exp(jnp.sum(lv[2:3] * lv[3:4], axis=1, keepdims=True)) + lambda_init)
            o = (o0 - lam * o1).T
            o = o * lax.rsqrt(jnp.mean(o * o, axis=-1, keepdims=True) + EPS)
            o = o * g_ref[...] * (1.0 - lambda_init)
        o_ref[0, qi * t:(qi + 1) * t, :] = o.astype(o_ref.dtype)


def _attention(mode, qt, k, vt, extra, lambda_init=0.0, side_casts=()):
    b, n_groups, _, s = qt.shape
    t = ATTN_TILE
    nt = s // t
    assert (nt * (nt - 1)) % (2 * ITEMS_PER_TRIP) == 0
    tab = jnp.asarray(_attn_items(nt), jnp.int32)
    in_specs = [
        pl.BlockSpec(memory_space=pltpu.SMEM),
        pl.BlockSpec((1, 1, LANES, s), lambda i, h: (i, h, 0, 0)),
        pl.BlockSpec((1, 1, s, LANES), lambda i, h: (i, h, 0, 0)),
        pl.BlockSpec((1, 1, LANES, s), lambda i, h: (i, h, 0, 0)),
    ]
    if mode == "fox":
        (kx,) = extra
        in_specs.append(pl.BlockSpec((1, s, LANES), lambda i, h: (i, 0, 0)))
        dv = HEAD_DIM
        pick_scratch = [pltpu.VMEM((2, LANES, t), BF16)]
    else:
        lamv, g = extra
        in_specs += [_const_spec(lamv.shape), _const_spec(g.shape)]
        dv = LANES
        pick_scratch = []
    side_specs = [pl.BlockSpec((1,) + w.shape[1:], lambda i, h: (i * n_groups + h, 0, 0)) for w in side_casts]
    assert all(w.shape[0] == b * n_groups for w in side_casts)
    outs = pl.pallas_call(
        functools.partial(_attn_kernel, mode=mode, lambda_init=lambda_init, n_side=len(side_casts)),
        grid=(b, n_groups),
        in_specs=in_specs + side_specs,
        out_specs=[pl.BlockSpec((1, s, LANES), lambda i, h: (i, 0, h))] + side_specs,
        out_shape=[jax.ShapeDtypeStruct((b, s, n_groups * LANES), BF16)]
        + [jax.ShapeDtypeStruct(w.shape, BF16) for w in side_casts],
        scratch_shapes=[
            *pick_scratch,
            pltpu.VMEM((t, t), F32),
            pltpu.VMEM((t, t), F32),
            pltpu.VMEM((t, t), F32),
            pltpu.VMEM((t, t), F32),
            pltpu.VMEM((2, 2, 1, t), F32),
            pltpu.VMEM((nt, 2, 1, t), F32),
            pltpu.VMEM((nt, 2, dv + ONES_ROWS, t), F32),
        ],
        compiler_params=_params("parallel", "parallel"),
        name=mode + "_attn",
    )(tab, qt, k, vt, *extra, *side_casts)
    return outs if side_casts else outs[0]


FF_CHUNK = 512


def _post_kernel(*refs, n_mix, final_norm, next_proj):
    h_ref = refs[0]
    ys = refs[1:1 + n_mix]
    ws = refs[1 + n_mix:1 + 2 * n_mix]
    rest = refs[1 + 2 * n_mix:]
    gx_ref, qk_ref, vo_ref, gm_ref, wu_ref, wd_ref, gf_ref = rest[:7]
    if next_proj:
        gn_ref, wn_ref, o_ref, nqt_ref, nk_ref, nvt_ref = rest[7:]
    else:
        (o_ref,) = rest[7:]

    h = h_ref[0]
    for y_ref, w_ref in zip(ys, ws):
        h = h + _dot(y_ref[0], w_ref[...])

    logits = _dot(_rms(h, gx_ref[...]).astype(BF16), qk_ref[...])
    probs = []
    for e in range(XATTN_HEADS):
        s = logits[:, e * N_MEM:(e + 1) * N_MEM]
        p = jnp.exp(s - jnp.max(s, axis=1, keepdims=True))
        probs.append((p / jnp.sum(p, axis=1, keepdims=True)).astype(BF16))
    h = h + _dot(jnp.concatenate(probs, axis=1), vo_ref[...])

    xn = _rms(h, gm_ref[...]).astype(BF16)
    for c in range(wu_ref.shape[1] // FF_CHUNK):
        sl = slice(c * FF_CHUNK, (c + 1) * FF_CHUNK)
        u = jnp.maximum(_dot(xn, wu_ref[:, sl]), 0.0)
        h = h + _dot((u * u).astype(BF16), wd_ref[sl, :])
    if final_norm:
        h = _rms(h, gf_ref[...])
    o_ref[0] = h
    if next_proj:
        _odd_proj_rows(h, gn_ref, wn_ref, nqt_ref, nk_ref, nvt_ref)


def _layer_spec(shape, layer, block=0):
    return pl.BlockSpec((None,) + tuple(shape), lambda *_: (layer, block, 0))


def _post(h, ys, w_mix, mix_layer, gx, qk, vo, gm, wu, wd, layer, gf, final_norm, next_proj=None):
    b, s, d = h.shape
    tm = ROW_TILE
    n_mix = len(ys)
    widths = [y.shape[2] for y in ys]
    assert all(wd_ == widths[0] for wd_ in widths) and sum(widths) == w_mix.shape[1]
    tile = lambda width: pl.BlockSpec((1, tm, width), lambda i, j: (i, j, 0))
    in_specs = [tile(d)] + [tile(wd_) for wd_ in widths]
    in_specs += [_layer_spec((widths[0], d), mix_layer, block=k) for k in range(n_mix)]
    in_specs += [
        _const_spec((1, d)),
        pl.BlockSpec((None, None, d, d), lambda i, j: (layer, i, 0, 0)),
        pl.BlockSpec((None, None, d, d), lambda i, j: (layer, i, 0, 0)),
        _const_spec((1, d)),
        _layer_spec(wu.shape[1:], layer),
        _layer_spec(wd.shape[1:], layer),
        _const_spec((1, d)),
    ]
    out_specs, out_shape, extra_in = [tile(d)], [jax.ShapeDtypeStruct((b, s, d), F32)], []
    if next_proj:
        in_specs += [_const_spec((1, d)), _const_spec(next_proj[1].shape)]
        extra_in = list(next_proj)
        qkv_specs, qkv_shapes = _qkv_specs(b, s, tm, DIFF_HEADS)
        out_specs += qkv_specs
        out_shape += qkv_shapes
    outs = pl.pallas_call(
        functools.partial(_post_kernel, n_mix=n_mix, final_norm=final_norm, next_proj=bool(next_proj)),
        grid=(b, s // tm),
        in_specs=in_specs,
        out_specs=out_specs,
        out_shape=out_shape,
        compiler_params=_params("parallel", "parallel"),
        name="post",
    )(h, *ys, *([w_mix] * n_mix), gx, qk, vo, gm, wu, wd, gf, *extra_in)
    return outs if next_proj else outs[0]


def _fold_xattn_kernel(mem_ref, g_ref, wkv_ref, wq_ref, wo_ref, qk_ref, vo_ref):
    d = wq_ref.shape[0]
    dh = d // XATTN_HEADS
    kv = _dot(_rms(mem_ref[...], g_ref[...]).astype(BF16), wkv_ref[...]).astype(BF16)
    for e in range(XATTN_HEADS):
        sl = slice(e * dh, (e + 1) * dh)
        qk_ref[:, e * N_MEM:(e + 1) * N_MEM] = (_dot_nt(wq_ref[:, sl], kv[:, sl]) * dh ** -0.5).astype(BF16)
        vo_ref[e * N_MEM:(e + 1) * N_MEM, :] = _dot(kv[:, d + e * dh:d + (e + 1) * dh], wo_ref[sl, :]).astype(BF16)


def _fold_xattn(mem, g, wkv, wq, wo):
    b, n_mem, d = mem.shape
    depth = wkv.shape[0]
    assert n_mem == N_MEM and XATTN_HEADS * N_MEM == d
    per_layer = lambda cols: pl.BlockSpec((None, d, cols), lambda l, i: (l, 0, 0))
    per_batch = pl.BlockSpec((None, None, d, d), lambda l, i: (l, i, 0, 0))
    return pl.pallas_call(
        _fold_xattn_kernel,
        grid=(depth, b),
        in_specs=[
            pl.BlockSpec((None, N_MEM, d), lambda l, i: (i, 0, 0)),
            _const_spec((1, d)),
            per_layer(2 * d),
            per_layer(d),
            per_layer(d),
        ],
        out_specs=[per_batch, per_batch],
        out_shape=[jax.ShapeDtypeStruct((depth, b, d, d), BF16)] * 2,
        compiler_params=_params("parallel", "parallel"),
        name="fold_xattn",
    )(mem, g, wkv, wq, wo)


def _block_diag(w):
    g, n, _ = w.shape
    eye = jnp.eye(g, dtype=w.dtype)
    return (eye[:, None, :, None] * w[:, :, None, :]).reshape(g * n, g * n)


def kernel(x, mem, g_mem, g_final, mix_norm_g, xattn_norm_g, mlp_norm_g, w_in_even, conv_w, conv_b, w_rgate, b_rgate, w_igate, b_igate, lru_lambda, fox_forget_b, w_out_even, w_in_odd, lambda_q1, lambda_k1, lambda_q2, lambda_k2, diff_norm_g, w_out_odd, xattn_wq, xattn_wkv, xattn_wo, w_up, w_down):
    b, s, d = x.shape
    depth = mix_norm_g.shape[0]
    row = lambda v: v.reshape(1, -1).astype(F32)

    late_weights = (w_up, w_down, w_out_even, w_out_odd, w_in_odd, xattn_wq, xattn_wo, xattn_wkv)
    late_bf16 = None

    h = x
    for layer in range(depth):
        i = layer // 2
        if layer % 2 == 0:
            w_in = w_in_even[i][:, :2560].astype(BF16)
            w_f = jnp.pad(w_in_even[i][:, 2560:], ((0, 0), (0, LANES - FOX_HEADS))).astype(BF16)
            zl, qt, k, vt, kx = _even_proj(h, row(mix_norm_g[layer]), w_in, w_f,
                                           fox_forget_b[i].reshape(FOX_HEADS, 1).astype(F32))
            wg = jnp.concatenate([_block_diag(w_rgate[i]), _block_diag(w_igate[i])], axis=1).astype(BF16)
            bg = jnp.concatenate([b_rgate[i], b_igate[i]]).reshape(1, -1).astype(F32)
            y_lru = _lru(zl, conv_w[i].astype(F32), row(conv_b[i]), wg, bg, row(lru_lambda[i]))
            if late_bf16 is None:
                slabs = b * (FOX_HEADS // 2)
                y_fox, *cast = _attention("fox", qt, k, vt, (kx,),
                                          side_casts=tuple(w.reshape(slabs, -1, w.shape[-1]) for w in late_weights))
                late_bf16 = [c.reshape(w.shape) for c, w in zip(cast, late_weights)]
                wu_all, wd_all, w_out_even_all, w_out_odd_all, w_in_odd_all, wq_all, wo_all, wkv_all = late_bf16
                qk_all, vo_all = _fold_xattn(mem, row(g_mem), wkv_all, wq_all, wo_all)
            else:
                y_fox = _attention("fox", qt, k, vt, (kx,))
            ys, w_mix = [y_lru, y_fox], w_out_even_all
        else:
            qt, k, vt = odd_inputs
            lambda_init = 0.8 - 0.6 * math.exp(-0.3 * layer)
            lamv = jnp.stack([lambda_q1[i], lambda_k1[i], lambda_q2[i], lambda_k2[i]]).astype(F32)
            y = _attention("diff", qt, k, vt, (lamv, row(diff_norm_g[i])), lambda_init=lambda_init)
            ys, w_mix = [y], w_out_odd_all

        next_proj = None
        if layer % 2 == 0 and layer + 1 < depth:
            next_proj = (row(mix_norm_g[layer + 1]), w_in_odd_all[(layer + 1) // 2])
        res = _post(h, ys, w_mix, i, row(xattn_norm_g[layer]), qk_all, vo_all,
                    row(mlp_norm_g[layer]), wu_all, wd_all, layer,
                    row(g_final), final_norm=(layer == depth - 1), next_proj=next_proj)
        if next_proj:
            h, *odd_inputs = res
        else:
            h = res
    return h
```
